```python
import math
import jax, jax.numpy as jnp
from jax import lax
import numpy as np

D_MODEL = 1024
BATCH = 4
SEQ = 8192
DEPTH = 2

N_EVEN = (DEPTH + 1) // 2
N_ODD = DEPTH // 2
EPS = 1e-6

MLA_HEADS = 16
QK_NOPE = 64
QK_ROPE = 32
V_DIM = 64
Q_LORA = 384
KV_LORA = 256
MLA_W = MLA_HEADS * V_DIM
ROPE_THETA = 10000.0
Q_BLOCK = 128

CONV_W = 1024
CONV_K = 31

SSD_HEAD_DIM = 64
SSD_HEADS = 24
SSD_W = SSD_HEADS * SSD_HEAD_DIM
SSD_GROUPS = 4
SSD_HPG = SSD_HEADS // SSD_GROUPS
SSD_STATE = 128
SSD_CONV_K = 5
SSD_CHUNK = 128
XBC_W = SSD_W + 2 * SSD_GROUPS * SSD_STATE

SC_W = 512
SC_K = 3

E_IN = Q_LORA + KV_LORA + QK_ROPE + MLA_W + 3 * CONV_W
O_IN = SSD_W + XBC_W + 2 * SSD_HEADS + 4 * SC_W

kernel_name = "hybrid_mla_conformer_ssd_shortconv_encoder"


def _offsets(*widths):
    out, acc = [], 0
    for w in widths[:-1]:
        acc += w
        out.append(acc)
    return out


def rms_norm(x, g):
    xf = x.astype(jnp.float32)
    y = xf * lax.rsqrt(jnp.mean(xf * xf, axis=-1, keepdims=True) + EPS)
    return (y * g.astype(jnp.float32)).astype(x.dtype)


def layer_norm(x, g, b):
    xf = x.astype(jnp.float32)
    xc = xf - jnp.mean(xf, axis=-1, keepdims=True)
    var = jnp.mean(xc * xc, axis=-1, keepdims=True)
    return (xc * lax.rsqrt(var + EPS) * g.astype(jnp.float32) + b.astype(jnp.float32)).astype(x.dtype)


def dw_conv(u, w):
    k = w.shape[0]
    return lax.conv_general_dilated(
        u, w[:, None, :].astype(u.dtype), window_strides=(1,), padding=[(k // 2, k // 2)],
        dimension_numbers=("NWC", "WIO", "NWC"), feature_group_count=u.shape[-1])


def rope_tables(seq):
    half = QK_ROPE // 2
    inv = ROPE_THETA ** (-jnp.arange(half, dtype=jnp.float32) / half)
    ang = jnp.arange(seq, dtype=jnp.float32)[:, None] * inv[None, :]
    return jnp.cos(ang), jnp.sin(ang)


def apply_rope(x, cos, sin):
    x1, x2 = jnp.split(x, 2, axis=-1)
    cos = cos.astype(x.dtype)
    sin = sin.astype(x.dtype)
    return jnp.concatenate([x1 * cos - x2 * sin, x2 * cos + x1 * sin], axis=-1)


def mla(q_c, kv_c, k_r, w_uq, q_g, w_ukv, kv_g, cos, sin):
    b, s, _ = q_c.shape
    q = (rms_norm(q_c, q_g) @ w_uq).reshape(b, s, MLA_HEADS, QK_NOPE + QK_ROPE)
    q_nope = q[..., :QK_NOPE]
    q_rope = apply_rope(q[..., QK_NOPE:], cos[None, :, None, :], sin[None, :, None, :])
    kv = (rms_norm(kv_c, kv_g) @ w_ukv).reshape(b, s, MLA_HEADS, QK_NOPE + V_DIM)
    k_nope, v = kv[..., :QK_NOPE], kv[..., QK_NOPE:]
    k_rope = apply_rope(k_r, cos[None], sin[None])
    scale = (QK_NOPE + QK_ROPE) ** -0.5
    nb = s // Q_BLOCK

    def to_blocks(t):
        return jnp.moveaxis(t.reshape(b, nb, Q_BLOCK, *t.shape[2:]), 1, 0)

    def attend(blk):
        qn, qr = blk
        sc = jnp.einsum("bqhd,bkhd->bhqk", qn, k_nope) + jnp.einsum("bqhr,bkr->bhqk", qr, k_rope)
        p = jax.nn.softmax(sc.astype(jnp.float32) * scale, axis=-1).astype(v.dtype)
        return jnp.einsum("bhqk,bkhv->bqhv", p, v)

    o = lax.map(attend, (to_blocks(q_nope), to_blocks(q_rope)))
    return jnp.moveaxis(o, 0, 1).reshape(b, s, MLA_W)


def conformer_conv(a, g, w, bias, ln_g, ln_b):
    u = a * jax.nn.sigmoid(g)
    u = dw_conv(u, w) + bias.astype(u.dtype)
    return jax.nn.silu(layer_norm(u, ln_g, ln_b))


def ssd_scan(xs, dt, a, bm, cm):
    b, s, g, r, p = xs.shape
    l = SSD_CHUNK
    c = s // l
    X = (xs * dt[..., None].astype(xs.dtype)).reshape(b, c, l, g, r, p)
    Bc = bm.reshape(b, c, l, g, -1)
    Cc = cm.reshape(b, c, l, g, -1)
    dA = (dt.astype(jnp.float32) * a.astype(jnp.float32)).reshape(b, c, l, g, r)
    a_cs = jnp.cumsum(jnp.moveaxis(dA, 2, -1), axis=-1)
    tri = jnp.tril(jnp.ones((l, l), dtype=bool))
    L = jnp.exp(jnp.where(tri, a_cs[..., :, None] - a_cs[..., None, :], -jnp.inf))
    cb = jnp.einsum("bclgn,bcsgn->bcgls", Cc, Bc)
    M = (cb[:, :, :, None].astype(jnp.float32) * L).astype(X.dtype)
    y_diag = jnp.einsum("bcgrls,bcsgrp->bclgrp", M, X)
    decay_states = jnp.exp(a_cs[..., -1:] - a_cs).astype(X.dtype)
    states = jnp.einsum("bclgn,bcgrl,bclgrp->bcgrpn", Bc, decay_states, X)
    chunk_cs = jnp.cumsum(jnp.pad(a_cs[..., -1], ((0, 0), (1, 0), (0, 0), (0, 0))), axis=1)
    ctri = jnp.tril(jnp.ones((c + 1, c + 1), dtype=bool))[None, :, :, None, None]
    decay_chunk = jnp.exp(jnp.where(ctri, chunk_cs[:, :, None] - chunk_cs[:, None, :], -jnp.inf)).astype(X.dtype)
    states = jnp.pad(states, ((0, 0), (1, 0), (0, 0), (0, 0), (0, 0), (0, 0)))
    entering = jnp.einsum("bzcgr,bcgrpn->bzgrpn", decay_chunk, states)[:, :-1]
    y_off = jnp.einsum("bclgn,bcgrpn,bcgrl->bclgrp", Cc, entering, jnp.exp(a_cs).astype(X.dtype))
    return (y_diag + y_off).reshape(b, s, g, r, p)


def ssd_mixer(z, xbc, dt_f_raw, dt_b_raw, conv_w, conv_b, dt_bias_f, dt_bias_b,
              a_log_f, a_log_b, d_skip, norm_g):
    b, s, _ = xbc.shape
    gn = SSD_GROUPS * SSD_STATE
    xbc = jax.nn.silu(dw_conv(xbc, conv_w) + conv_b.astype(xbc.dtype))
    xs = xbc[..., :SSD_W].reshape(b, s, SSD_GROUPS, SSD_HPG, SSD_HEAD_DIM)
    bm = xbc[..., SSD_W:SSD_W + gn].reshape(b, s, SSD_GROUPS, SSD_STATE)
    cm = xbc[..., SSD_W + gn:].reshape(b, s, SSD_GROUPS, SSD_STATE)
    dt_f = jax.nn.softplus(dt_f_raw + dt_bias_f).reshape(b, s, SSD_GROUPS, SSD_HPG)
    dt_b = jax.nn.softplus(dt_b_raw + dt_bias_b).reshape(b, s, SSD_GROUPS, SSD_HPG)
    a_f = -jnp.exp(a_log_f).reshape(SSD_GROUPS, SSD_HPG)
    a_b = -jnp.exp(a_log_b).reshape(SSD_GROUPS, SSD_HPG)
    flip = lambda t: jnp.flip(t, axis=1)
    y_fwd = ssd_scan(xs, dt_f, a_f, bm, cm)
    y_bwd = flip(ssd_scan(flip(xs), flip(dt_b), a_b, flip(bm), flip(cm)))
    y = y_fwd + y_bwd + d_skip.reshape(SSD_GROUPS, SSD_HPG, 1).astype(xs.dtype) * xs
    y = y.reshape(b, s, SSD_W) * jax.nn.silu(z)
    gw = SSD_W // SSD_GROUPS
    y = rms_norm(y.reshape(b, s, SSD_GROUPS, gw), norm_g.reshape(SSD_GROUPS, gw))
    return y.reshape(b, s, SSD_W)


def short_conv(g_b, g_c, h, w):
    return g_b * dw_conv(g_c * h, w)


def even_layer(x, norm_g, w_in, w_uq, q_g, w_ukv, kv_g, conv_w, conv_b, ln_g, ln_b, w_out, cos, sin):
    u = rms_norm(x, norm_g) @ w_in
    q_c, kv_c, k_r, z_b, glu_a, glu_g, z_a = jnp.split(
        u, _offsets(Q_LORA, KV_LORA, QK_ROPE, MLA_W, CONV_W, CONV_W, CONV_W), axis=-1)
    o_b = mla(q_c, kv_c, k_r, w_uq, q_g, w_ukv, kv_g, cos, sin) * jax.nn.silu(z_b)
    o_a = conformer_conv(glu_a, glu_g, conv_w, conv_b, ln_g, ln_b) * jax.nn.silu(z_a)
    return x + (jnp.concatenate([o_b, o_a], axis=-1) @ w_out).astype(x.dtype)


def odd_layer(x, norm_g, w_in, conv_c_w, conv_c_b, dt_bias_f, dt_bias_b, a_log_f, a_log_b,
              d_skip, ssd_g, conv_d_w, w_out):
    u = rms_norm(x, norm_g) @ w_in
    z_c, xbc, dt_f, dt_b, g_b, g_c, h_d, z_d = jnp.split(
        u, _offsets(SSD_W, XBC_W, SSD_HEADS, SSD_HEADS, SC_W, SC_W, SC_W, SC_W), axis=-1)
    o_c = ssd_mixer(z_c, xbc, dt_f, dt_b, conv_c_w, conv_c_b, dt_bias_f, dt_bias_b,
                    a_log_f, a_log_b, d_skip, ssd_g)
    o_d = short_conv(g_b, g_c, h_d, conv_d_w) * jax.nn.silu(z_d)
    return x + (jnp.concatenate([o_c, o_d], axis=-1) @ w_out).astype(x.dtype)


def setup_inputs(seed: int = 0) -> dict:
    key = jax.random.key(seed)
    ks = iter(jax.random.split(key, 32))
    nrm = lambda shape, scale: jax.random.normal(next(ks), shape, jnp.float32) * scale
    gain = lambda shape: 1.0 + 0.05 * jax.random.normal(next(ks), shape, jnp.float32)
    NE, NO = N_EVEN, N_ODD

    def dt_bias(shape):
        dt = jnp.exp(jax.random.uniform(next(ks), shape, jnp.float32,
                                        minval=math.log(1e-3), maxval=math.log(1e-1)))
        return dt + jnp.log(-jnp.expm1(-dt))

    def a_log(shape):
        return jnp.log(jax.random.uniform(next(ks), shape, jnp.float32, minval=1.0, maxval=16.0))

    return {
        "x": nrm((BATCH, SEQ, D_MODEL), 1.0),
        "norm_e": gain((NE, D_MODEL)),
        "w_in_e": nrm((NE, D_MODEL, E_IN), D_MODEL ** -0.5),
        "w_uq": nrm((NE, Q_LORA, MLA_HEADS * (QK_NOPE + QK_ROPE)), Q_LORA ** -0.5),
        "q_norm": gain((NE, Q_LORA)),
        "w_ukv": nrm((NE, KV_LORA, MLA_HEADS * (QK_NOPE + V_DIM)), KV_LORA ** -0.5),
        "kv_norm": gain((NE, KV_LORA)),
        "conv_a_w": nrm((NE, CONV_K, CONV_W), CONV_K ** -0.5),
        "conv_a_b": nrm((NE, CONV_W), 0.02),
        "ln_a_g": gain((NE, CONV_W)),
        "ln_a_b": nrm((NE, CONV_W), 0.02),
        "w_out_e": nrm((NE, MLA_W + CONV_W, D_MODEL), (MLA_W + CONV_W) ** -0.5),
        "norm_o": gain((NO, D_MODEL)),
        "w_in_o": nrm((NO, D_MODEL, O_IN), D_MODEL ** -0.5),
        "conv_c_w": nrm((NO, SSD_CONV_K, XBC_W), SSD_CONV_K ** -0.5),
        "conv_c_b": nrm((NO, XBC_W), 0.02),
        "dt_bias_f": dt_bias((NO, SSD_HEADS)),
        "dt_bias_b": dt_bias((NO, SSD_HEADS)),
        "a_log_f": a_log((NO, SSD_HEADS)),
        "a_log_b": a_log((NO, SSD_HEADS)),
        "d_skip": gain((NO, SSD_HEADS)),
        "ssd_norm": gain((NO, SSD_W)),
        "conv_d_w": nrm((NO, SC_K, SC_W), SC_K ** -0.5),
        "w_out_o": nrm((NO, SSD_W + SC_W, D_MODEL), (SSD_W + SC_W) ** -0.5),
        "final_norm": gain((D_MODEL,)),
    }


def reference(x, norm_e, w_in_e, w_uq, q_norm, w_ukv, kv_norm, conv_a_w, conv_a_b, ln_a_g, ln_a_b,
              w_out_e, norm_o, w_in_o, conv_c_w, conv_c_b, dt_bias_f, dt_bias_b, a_log_f, a_log_b,
              d_skip, ssd_norm, conv_d_w, w_out_o, final_norm):
    cos, sin = rope_tables(x.shape[1])
    for i in range(DEPTH):
        j = i // 2
        if i % 2 == 0:
            x = even_layer(x, norm_e[j], w_in_e[j], w_uq[j], q_norm[j], w_ukv[j], kv_norm[j],
                           conv_a_w[j], conv_a_b[j], ln_a_g[j], ln_a_b[j], w_out_e[j], cos, sin)
        else:
            x = odd_layer(x, norm_o[j], w_in_o[j], conv_c_w[j], conv_c_b[j], dt_bias_f[j], dt_bias_b[j],
                          a_log_f[j], a_log_b[j], d_skip[j], ssd_norm[j], conv_d_w[j], w_out_o[j])
    return rms_norm(x, final_norm)
```

```python
import functools
import math

import jax
import jax.numpy as jnp
from jax import lax
from jax.experimental import pallas as pl
from jax.experimental.pallas import tpu as pltpu

F32 = jnp.float32
BF16 = jnp.bfloat16
EPS = 1e-6

D_MODEL = 1024
MLA_HEADS = 16
QK_NOPE = 64
QK_ROPE = 32
V_DIM = 64
Q_LORA = 384
KV_LORA = 256
MLA_W = MLA_HEADS * V_DIM
ROPE_THETA = 10000.0
CONV_W = 1024
CONV_K = 31
SSD_HEAD_DIM = 64
SSD_HEADS = 24
SSD_W = SSD_HEADS * SSD_HEAD_DIM
SSD_GROUPS = 4
SSD_HPG = SSD_HEADS // SSD_GROUPS
SSD_STATE = 128
SSD_CONV_K = 5
SSD_CHUNK = 128
XBC_W = SSD_W + 2 * SSD_GROUPS * SSD_STATE
SC_W = 512
SC_K = 3

LANES = 128
HALO = 16
QK_PAD = 128
VT_ROWS = 80
VMEM_LIMIT = 56 * 1024 * 1024


def _cparams(sem):
    return pltpu.CompilerParams(dimension_semantics=sem, vmem_limit_bytes=VMEM_LIMIT)


def _silu(v):
    return v * (1.0 / (1.0 + jnp.exp(-v)))


def _softplus(v):
    return jnp.maximum(v, 0.0) + jnp.log(1.0 + jnp.exp(-jnp.abs(v)))


def _rms(v, g):
    ms = jnp.mean(v * v, axis=-1, keepdims=True)
    return v * lax.rsqrt(ms + EPS) * g


def _rms_matmul_kernel(x_ref, g_ref, w_ref, o_ref, xn_ref):
    @pl.when(pl.program_id(1) == 0)
    def _():
        xn_ref[...] = _rms(x_ref[...], g_ref[...]).astype(xn_ref.dtype)

    o_ref[...] = jnp.dot(xn_ref[...], w_ref[...], preferred_element_type=F32).astype(o_ref.dtype)


def _rms_matmul(x, g, w, tm, tn, out_dtype):
    n, d = x.shape
    e = w.shape[1]
    return pl.pallas_call(
        _rms_matmul_kernel,
        grid=(n // tm, e // tn),
        in_specs=[
            pl.BlockSpec((tm, d), lambda i, j: (i, 0)),
            pl.BlockSpec((1, d), lambda i, j: (0, 0)),
            pl.BlockSpec((d, tn), lambda i, j: (0, j)),
        ],
        out_specs=pl.BlockSpec((tm, tn), lambda i, j: (i, j)),
        out_shape=jax.ShapeDtypeStruct((n, e), out_dtype),
        scratch_shapes=[pltpu.VMEM((tm, d), BF16)],
        compiler_params=_cparams(("parallel", "arbitrary")),
        name="rms_matmul",
    )(x, g, w)


def _mla_prep_kernel(x_ref, g_ref, wa_ref, qg_ref, kvg_ref, wqt_ref, wk_ref, place_ref, wvt_ref,
                     ones_ref, cos2_ref, sin2_ref, cost_ref, sint_ref,
                     qt_ref, k_ref, vt_ref, *, qscale):
    xn = _rms(x_ref[...], g_ref[...]).astype(BF16)
    a = jnp.dot(xn, wa_ref[...], preferred_element_type=F32)
    qn = _rms(a[:, :Q_LORA], qg_ref[...]).astype(BF16)
    kvn = _rms(a[:, Q_LORA:Q_LORA + KV_LORA], kvg_ref[...]).astype(BF16)

    kr = a[:, Q_LORA + KV_LORA:]
    lane = lax.broadcasted_iota(jnp.int32, kr.shape, 1)
    half = QK_ROPE // 2
    swapped = jnp.where(lane < half, pltpu.roll(kr, LANES - half, 1), pltpu.roll(kr, half, 1))
    kr = kr * cos2_ref[...] + swapped * sin2_ref[...]

    k = jnp.dot(kvn, wk_ref[...], preferred_element_type=F32)
    k = k + jnp.dot(kr.astype(BF16), place_ref[...], preferred_element_type=F32)
    k_ref[...] = k.astype(k_ref.dtype)

    nt = (((1,), (1,)), ((), ()))
    vt = lax.dot_general(wvt_ref[...], kvn, nt, preferred_element_type=F32) + ones_ref[...]
    vt_ref[...] = vt.astype(vt_ref.dtype)

    qt = lax.dot_general(wqt_ref[...], qn, nt, preferred_element_type=F32) * qscale
    qt_ref[...] = qt.astype(qt_ref.dtype)
    cos_t = cost_ref[...]
    sin_t = sint_ref[...]
    for h in range(MLA_HEADS):
        r0 = h * QK_PAD + QK_NOPE
        x1 = qt[r0:r0 + half]
        x2 = qt[r0 + half:r0 + 2 * half]
        qt_ref[r0:r0 + half, :] = (x1 * cos_t - x2 * sin_t).astype(qt_ref.dtype)
        qt_ref[r0 + half:r0 + 2 * half, :] = (x2 * cos_t + x1 * sin_t).astype(qt_ref.dtype)


def _mla_prep(x2d, g, wa, qg, kvg, wqt, wk, place, wvt, ones_col, cos2, sin2, cos_t, sin_t,
              batch, seq, tm):
    nblk = seq // tm
    qscale = (QK_NOPE + QK_ROPE) ** -0.5 * math.log2(math.e)
    full = lambda arr: pl.BlockSpec(arr.shape, lambda b, i: (0,) * arr.ndim)
    hq = MLA_HEADS * QK_PAD
    hv = MLA_HEADS * VT_ROWS
    return pl.pallas_call(
        functools.partial(_mla_prep_kernel, qscale=qscale),
        grid=(batch, nblk),
        in_specs=[
            pl.BlockSpec((tm, D_MODEL), lambda b, i: (b * nblk + i, 0)),
            full(g), full(wa), full(qg), full(kvg), full(wqt), full(wk), full(place), full(wvt),
            full(ones_col),
            pl.BlockSpec((tm, LANES), lambda b, i: (i, 0)),
            pl.BlockSpec((tm, LANES), lambda b, i: (i, 0)),
            pl.BlockSpec((QK_ROPE // 2, tm), lambda b, i: (0, i)),
            pl.BlockSpec((QK_ROPE // 2, tm), lambda b, i: (0, i)),
        ],
        out_specs=[
            pl.BlockSpec((None, hq, tm), lambda b, i: (b, 0, i)),
            pl.BlockSpec((None, tm, hq), lambda b, i: (b, i, 0)),
            pl.BlockSpec((None, None, hv, tm), lambda b, i: (b, i, 0, 0)),
        ],
        out_shape=[
            jax.ShapeDtypeStruct((batch, hq, seq), BF16),
            jax.ShapeDtypeStruct((batch, seq, hq), BF16),
            jax.ShapeDtypeStruct((batch, nblk, hv, tm), BF16),
        ],
        compiler_params=_cparams(("parallel", "parallel")),
        name="mla_prep",
    )(x2d, g, wa, qg, kvg, wqt, wk, place, wvt, ones_col, cos2, sin2, cos_t, sin_t)


def _attn_kernel(qt_ref, k_ref, vt_ref, o_ref, m_ref, acc_ref, *, tk, nk):
    qt = qt_ref[...]
    m_ref[...] = jnp.full(m_ref.shape, -jnp.inf, F32)
    acc_ref[...] = jnp.zeros(acc_ref.shape, F32)

    def body(c, carry):
        k0 = pl.multiple_of(c * tk, tk)
        st = jnp.dot(k_ref[pl.ds(k0, tk), :], qt, preferred_element_type=F32)
        m_old = m_ref[...]
        m_new = jnp.maximum(m_old, jnp.max(st, axis=0, keepdims=True))
        p = jnp.exp2(st - m_new).astype(BF16)
        pv = jnp.dot(vt_ref[c], p, preferred_element_type=F32)
        acc_ref[...] = acc_ref[...] * jnp.exp2(m_old - m_new) + pv
        m_ref[...] = m_new
        return carry

    lax.fori_loop(0, nk, body, 0)
    acc = acc_ref[...]
    o_ref[...] = (acc[:V_DIM] * (1.0 / acc[V_DIM:V_DIM + 1])).astype(o_ref.dtype)


def _attention(qt, k, vt, batch, seq, tq, tk):
    nk = seq // tk
    return pl.pallas_call(
        functools.partial(_attn_kernel, tk=tk, nk=nk),
        grid=(batch, MLA_HEADS, seq // tq),
        in_specs=[
            pl.BlockSpec((None, QK_PAD, tq), lambda b, h, i: (b, h, i)),
            pl.BlockSpec((None, seq, QK_PAD), lambda b, h, i: (b, 0, h)),
            pl.BlockSpec((None, nk, VT_ROWS, tk), lambda b, h, i: (b, 0, h, 0)),
        ],
        out_specs=pl.BlockSpec((None, V_DIM, tq), lambda b, h, i: (b, h, i)),
        out_shape=jax.ShapeDtypeStruct((batch, MLA_W, seq), BF16),
        scratch_shapes=[pltpu.VMEM((1, tq), F32), pltpu.VMEM((VT_ROWS, tq), F32)],
        compiler_params=_cparams(("parallel", "parallel", "arbitrary")),
        name="mla_attention",
    )(qt, k, vt)


def _fill_ext(ext_ref, prev, cur, nxt, i, n_i, ts):
    ext_ref[0:HALO, :] = jnp.where(i > 0, prev, 0.0)
    ext_ref[HALO:HALO + ts, :] = cur
    ext_ref[HALO + ts:2 * HALO + ts, :] = jnp.where(i < n_i - 1, nxt, 0.0)


def _dwconv(ext_ref, w_ref, k_taps, ts, width, emit, rows=64):
    sub = 8
    off = HALO - k_taps // 2
    lo = (off // sub) * sub
    win_rows = ((off - lo + k_taps - 1) // sub + 1) * sub + rows

    def body(rc, carry):
        r0 = pl.multiple_of(rc * rows, rows)
        for cb in range(width // LANES):
            sl = slice(cb * LANES, (cb + 1) * LANES)
            win = ext_ref[pl.ds(r0 + lo, win_rows), sl]
            acc = jnp.zeros((rows, LANES), F32)
            for phase in range(sub):
                taps = [k for k in range(k_taps) if (off - lo + k) % sub == phase]
                if not taps:
                    continue
                shifted = win[phase:phase + win_rows - sub]
                for k in taps:
                    q = (off - lo + k) // sub * sub
                    acc = acc + w_ref[k:k + 1, sl] * shifted[q:q + rows]
            emit(r0, sl, acc)
        return carry

    lax.fori_loop(0, ts // rows, body, 0)


def _halo_specs(ts, width, col, nblk, batch):
    per = ts // HALO
    last = batch * nblk * per - 1

    def prev_map(b, i, *_):
        return (jnp.maximum((b * nblk + i) * per - 1, 0), col(*_))

    def cur_map(b, i, *_):
        return (b * nblk + i, col(*_))

    def next_map(b, i, *_):
        return (jnp.minimum((b * nblk + i + 1) * per, last), col(*_))

    return [pl.BlockSpec((HALO, width), prev_map), pl.BlockSpec((ts, width), cur_map),
            pl.BlockSpec((HALO, width), next_map)]


def _conformer_kernel(ap_ref, a_ref, an_ref, gp_ref, g_ref, gn_ref, z_ref, w_ref, b_ref, lng_ref,
                      lnb_ref, o_ref, ext_ref, conv_ref, *, ts):
    i = pl.program_id(1)
    n_i = pl.num_programs(1)

    def glu(a, g):
        return a.astype(F32) * (1.0 / (1.0 + jnp.exp(-g.astype(F32))))

    _fill_ext(ext_ref, glu(ap_ref[...], gp_ref[...]), glu(a_ref[...], g_ref[...]),
              glu(an_ref[...], gn_ref[...]), i, n_i, ts)

    def emit(r0, sl, acc):
        conv_ref[pl.ds(r0, acc.shape[0]), sl] = acc + b_ref[:, sl]

    _dwconv(ext_ref, w_ref, CONV_K, ts, CONV_W, emit)

    rows = 128

    def ln_body(rc, carry):
        r0 = pl.multiple_of(rc * rows, rows)
        v = conv_ref[pl.ds(r0, rows), :]
        vc = v - jnp.mean(v, axis=-1, keepdims=True)
        var = jnp.mean(vc * vc, axis=-1, keepdims=True)
        y = vc * lax.rsqrt(var + EPS) * lng_ref[...] + lnb_ref[...]
        y = _silu(y) * _silu(z_ref[pl.ds(r0, rows), :].astype(F32))
        o_ref[pl.ds(r0, rows), :] = y.astype(o_ref.dtype)
        return carry

    lax.fori_loop(0, ts // rows, ln_body, 0)


def _conformer(u2, w, b, lng, lnb, batch, seq, ts):
    n = batch * seq
    nblk = seq // ts
    a_specs = _halo_specs(ts, CONV_W, lambda: 1, nblk, batch)
    g_specs = _halo_specs(ts, CONV_W, lambda: 2, nblk, batch)
    small = lambda arr: pl.BlockSpec(arr.shape, lambda b_, i: (0, 0))
    return pl.pallas_call(
        functools.partial(_conformer_kernel, ts=ts),
        grid=(batch, nblk),
        in_specs=a_specs + g_specs + [
            pl.BlockSpec((ts, CONV_W), lambda b_, i: (b_ * nblk + i, 3)),
            small(w), small(b), small(lng), small(lnb)],
        out_specs=pl.BlockSpec((ts, CONV_W), lambda b_, i: (b_ * nblk + i, 0)),
        out_shape=jax.ShapeDtypeStruct((n, CONV_W), BF16),
        scratch_shapes=[pltpu.VMEM((ts + 2 * HALO, CONV_W), F32), pltpu.VMEM((ts, CONV_W), F32)],
        compiler_params=_cparams(("parallel", "parallel")),
        name="conformer_conv",
    )(u2, u2, u2, u2, u2, u2, u2, w, b, lng, lnb)


def _out_even_kernel(x_ref, ot_ref, z_ref, oa_ref, w1_ref, w2_ref, o_ref):
    ob = ot_ref[...].astype(F32).T * _silu(z_ref[...].astype(F32))
    acc = jnp.dot(ob.astype(BF16), w1_ref[...], preferred_element_type=F32)
    acc = acc + jnp.dot(oa_ref[...], w2_ref[...], preferred_element_type=F32)
    o_ref[...] = x_ref[...] + acc


def _out_even(x2d, ot, u2, oa, w1, w2, batch, seq, tm):
    n = batch * seq
    nblk = seq // tm
    return pl.pallas_call(
        _out_even_kernel,
        grid=(batch, nblk),
        in_specs=[
            pl.BlockSpec((tm, D_MODEL), lambda b, i: (b * nblk + i, 0)),
            pl.BlockSpec((None, MLA_W, tm), lambda b, i: (b, 0, i)),
            pl.BlockSpec((tm, MLA_W), lambda b, i: (b * nblk + i, 0)),
            pl.BlockSpec((tm, CONV_W), lambda b, i: (b * nblk + i, 0)),
            pl.BlockSpec(w1.shape, lambda b, i: (0, 0)),
            pl.BlockSpec(w2.shape, lambda b, i: (0, 0)),
        ],
        out_specs=pl.BlockSpec((tm, D_MODEL), lambda b, i: (b * nblk + i, 0)),
        out_shape=jax.ShapeDtypeStruct((n, D_MODEL), F32),
        compiler_params=_cparams(("parallel", "parallel")),
        name="out_proj_even",
    )(x2d, ot, u2, oa, w1, w2)


def _dt_kernel(x_ref, g_ref, w_ref, o_ref):
    xn = _rms(x_ref[...], g_ref[...]).astype(BF16)
    o_ref[...] = lax.dot_general(w_ref[...], xn, (((1,), (1,)), ((), ())), preferred_element_type=F32)


def _dt_proj(x2d, g, wdt_t, tm):
    n = x2d.shape[0]
    rows = wdt_t.shape[0]
    return pl.pallas_call(
        _dt_kernel,
        grid=(n // tm,),
        in_specs=[pl.BlockSpec((tm, D_MODEL), lambda i: (i, 0)),
                  pl.BlockSpec((1, D_MODEL), lambda i: (0, 0)),
                  pl.BlockSpec(wdt_t.shape, lambda i: (0, 0))],
        out_specs=pl.BlockSpec((rows, tm), lambda i: (0, i)),
        out_shape=jax.ShapeDtypeStruct((rows, n), F32),
        compiler_params=_cparams(("parallel",)),
        name="dt_proj",
    )(x2d, g, wdt_t)


def _xbc_conv_kernel(p_ref, c_ref, n_ref, w_ref, b_ref, o_ref, ext_ref, *, ts, width):
    i = pl.program_id(1)
    _fill_ext(ext_ref, p_ref[...].astype(F32), c_ref[...].astype(F32), n_ref[...].astype(F32),
              i, pl.num_programs(1), ts)

    def emit(r0, sl, acc):
        o_ref[pl.ds(r0, acc.shape[0]), sl] = _silu(acc + b_ref[:, sl]).astype(o_ref.dtype)

    _dwconv(ext_ref, w_ref, SSD_CONV_K, ts, width, emit)


def _xbc_conv(uo, w, b, batch, seq, ts, width, col0):
    n = batch * seq
    nblk = seq // ts
    ncol = XBC_W // width
    specs = _halo_specs(ts, width, lambda j: col0 + j, nblk, batch)
    return pl.pallas_call(
        functools.partial(_xbc_conv_kernel, ts=ts, width=width),
        grid=(batch, nblk, ncol),
        in_specs=specs + [pl.BlockSpec((SSD_CONV_K, width), lambda b_, i, j: (0, j)),
                          pl.BlockSpec((1, width), lambda b_, i, j: (0, j))],
        out_specs=pl.BlockSpec((ts, width), lambda b_, i, j: (b_ * nblk + i, j)),
        out_shape=jax.ShapeDtypeStruct((n, XBC_W), BF16),
        scratch_shapes=[pltpu.VMEM((ts + 2 * HALO, width), F32)],
        compiler_params=_cparams(("parallel", "parallel", "parallel")),
        name="xbc_conv",
    )(uo, uo, uo, w, b)


def _ssd_kernel(xs_ref, b_ref, c_ref, dt_ref, bias_ref, alog_ref, y_ref, state_ref):
    d = pl.program_id(2)
    c = pl.program_id(3)
    L = SSD_CHUNK
    pairs = SSD_HPG // 2

    @pl.when(c == 0)
    def _():
        state_ref[...] = jnp.zeros(state_ref.shape, F32)

    row = lax.broadcasted_iota(jnp.int32, (L, L), 0)
    col = lax.broadcasted_iota(jnp.int32, (L, L), 1)
    lane_lo = col < SSD_HEAD_DIM
    ahead = (row - col) * (1 - 2 * d)

    dtv = _softplus(dt_ref[...] + bias_ref[...])
    da = dtv * (-jnp.exp(alog_ref[...]))
    tri = jnp.where(ahead <= 0, 1.0, 0.0)
    cs_t = jnp.dot(da, tri, preferred_element_type=F32, precision=lax.Precision.HIGHEST)
    tot = jnp.sum(da, axis=1, keepdims=True)
    etot = jnp.exp(jnp.broadcast_to(tot, cs_t.shape))
    w_t = jnp.exp(tot - cs_t) * dtv
    ecs_t = jnp.exp(cs_t)
    stack = jnp.concatenate([cs_t, ecs_t, jnp.zeros((L - 16, L), F32)], axis=0).T

    mask = ahead >= 0
    bm = b_ref[...]
    cm = c_ref[...]
    cb = lax.dot_general(cm, bm, (((1,), (1,)), ((), ())), preferred_element_type=F32)
    bt = bm.astype(F32).T
    cmf = cm.astype(F32)

    for j in range(pairs):
        xs_pair = xs_ref[:, j * LANES:(j + 1) * LANES]
        st_pair = state_ref[j]
        rhs = jnp.concatenate([xs_pair, st_pair.astype(BF16)], axis=0)
        ys = []
        news = []
        for r in (2 * j, 2 * j + 1):
            e = stack[:, r:r + 1] - cs_t[r:r + 1, :]
            m = jnp.where(mask, jnp.exp(e), 0.0) * cb * dtv[r:r + 1, :]
            cs_scaled = cmf * stack[:, 8 + r:9 + r]
            lhs = jnp.concatenate([m.astype(BF16), cs_scaled.astype(BF16)], axis=1)
            ys.append(jnp.dot(lhs, rhs, preferred_element_type=F32))
            btw = (bt * w_t[r:r + 1, :]).astype(BF16)
            news.append(jnp.dot(btw, xs_pair, preferred_element_type=F32))
        y_ref[:, j * LANES:(j + 1) * LANES] = jnp.where(lane_lo, ys[0], ys[1]).astype(y_ref.dtype)
        decay = jnp.where(lane_lo[:1], etot[2 * j:2 * j + 1, :], etot[2 * j + 1:2 * j + 2, :])
        state_ref[j] = st_pair * decay + jnp.where(lane_lo, news[0], news[1])


def _ssd(xbc, dt_t, bias, alog, batch, seq):
    n = batch * seq
    nc = seq // SSD_CHUNK
    gw = SSD_HPG * SSD_HEAD_DIM
    b_col0 = SSD_W // SSD_STATE
    c_col0 = b_col0 + SSD_GROUPS

    def tpos(b, d, c):
        return b * nc + jnp.where(d == 0, c, nc - 1 - c)

    return pl.pallas_call(
        _ssd_kernel,
        grid=(batch, SSD_GROUPS, 2, nc),
        in_specs=[
            pl.BlockSpec((SSD_CHUNK, gw), lambda b, g, d, c: (tpos(b, d, c), g)),
            pl.BlockSpec((SSD_CHUNK, SSD_STATE), lambda b, g, d, c: (tpos(b, d, c), b_col0 + g)),
            pl.BlockSpec((SSD_CHUNK, SSD_STATE), lambda b, g, d, c: (tpos(b, d, c), c_col0 + g)),
            pl.BlockSpec((None, 8, SSD_CHUNK), lambda b, g, d, c: (d * SSD_GROUPS + g, 0, tpos(b, d, c))),
            pl.BlockSpec((None, 8, SSD_CHUNK), lambda b, g, d, c: (d * SSD_GROUPS + g, 0, 0)),
            pl.BlockSpec((None, 8, SSD_CHUNK), lambda b, g, d, c: (d * SSD_GROUPS + g, 0, 0)),
        ],
        out_specs=pl.BlockSpec((None, SSD_CHUNK, gw), lambda b, g, d, c: (d, tpos(b, d, c), g)),
        out_shape=jax.ShapeDtypeStruct((2, n, SSD_W), BF16),
        scratch_shapes=[pltpu.VMEM((SSD_HPG // 2, SSD_STATE, LANES), F32)],
        compiler_params=_cparams(("parallel", "parallel", "arbitrary", "arbitrary")),
        name="ssd_scan",
    )(xbc, xbc, xbc, dt_t, bias, alog)


def _ssd_gate_kernel(y_ref, xs_ref, z_ref, dskip_ref, g_ref, o_ref):
    gw = SSD_HPG * SSD_HEAD_DIM
    y = y_ref[0].astype(F32) + y_ref[1].astype(F32) + dskip_ref[...] * xs_ref[...].astype(F32)
    y = y * _silu(z_ref[...].astype(F32))
    for g in range(SSD_GROUPS):
        sl = slice(g * gw, (g + 1) * gw)
        o_ref[:, sl] = _rms(y[:, sl], g_ref[:, sl]).astype(o_ref.dtype)


def _ssd_gate(y2, xbc, uo, dskip, g, tm):
    n = xbc.shape[0]
    return pl.pallas_call(
        _ssd_gate_kernel,
        grid=(n // tm,),
        in_specs=[
            pl.BlockSpec((2, tm, SSD_W), lambda i: (0, i, 0)),
            pl.BlockSpec((tm, SSD_W), lambda i: (i, 0)),
            pl.BlockSpec((tm, SSD_W), lambda i: (i, 0)),
            pl.BlockSpec((1, SSD_W), lambda i: (0, 0)),
            pl.BlockSpec((1, SSD_W), lambda i: (0, 0)),
        ],
        out_specs=pl.BlockSpec((tm, SSD_W), lambda i: (i, 0)),
        out_shape=jax.ShapeDtypeStruct((n, SSD_W), BF16),
        compiler_params=_cparams(("parallel",)),
        name="ssd_gate_norm",
    )(y2, xbc, uo, dskip, g)


def _short_conv_kernel(gcp_ref, gc_ref, gcn_ref, hp_ref, h_ref, hn_ref, gb_ref, z_ref, w_ref,
                       o_ref, ext_ref, *, ts):
    i = pl.program_id(1)
    mul = lambda a, b: a.astype(F32) * b.astype(F32)
    _fill_ext(ext_ref, mul(gcp_ref[...], hp_ref[...]), mul(gc_ref[...], h_ref[...]),
              mul(gcn_ref[...], hn_ref[...]), i, pl.num_programs(1), ts)

    def emit(r0, sl, acc):
        rs = pl.ds(r0, acc.shape[0])
        y = gb_ref[rs, sl].astype(F32) * acc * _silu(z_ref[rs, sl].astype(F32))
        o_ref[rs, sl] = y.astype(o_ref.dtype)

    _dwconv(ext_ref, w_ref, SC_K, ts, SC_W, emit)


def _short_conv(uo, w, batch, seq, ts, col0):
    n = batch * seq
    nblk = seq // ts
    gc_specs = _halo_specs(ts, SC_W, lambda: col0 + 1, nblk, batch)
    h_specs = _halo_specs(ts, SC_W, lambda: col0 + 2, nblk, batch)
    return pl.pallas_call(
        functools.partial(_short_conv_kernel, ts=ts),
        grid=(batch, nblk),
        in_specs=gc_specs + h_specs + [
            pl.BlockSpec((ts, SC_W), lambda b, i: (b * nblk + i, col0)),
            pl.BlockSpec((ts, SC_W), lambda b, i: (b * nblk + i, col0 + 3)),
            pl.BlockSpec(w.shape, lambda b, i: (0, 0))],
        out_specs=pl.BlockSpec((ts, SC_W), lambda b, i: (b * nblk + i, 0)),
        out_shape=jax.ShapeDtypeStruct((n, SC_W), BF16),
        scratch_shapes=[pltpu.VMEM((ts + 2 * HALO, SC_W), F32)],
        compiler_params=_cparams(("parallel", "parallel")),
        name="short_conv",
    )(uo, uo, uo, uo, uo, uo, uo, uo, w)


def _out_odd_kernel(x_ref, oc_ref, od_ref, w1_ref, w2_ref, *rest):
    o_ref = rest[-1]
    acc = jnp.dot(oc_ref[...], w1_ref[...], preferred_element_type=F32)
    acc = acc + jnp.dot(od_ref[...], w2_ref[...], preferred_element_type=F32)
    y = x_ref[...] + acc
    if len(rest) == 2:
        y = _rms(y, rest[0][...])
    o_ref[...] = y


def _out_odd(x2d, oc, od, w1, w2, final_g, tm):
    n = x2d.shape[0]
    extra = [] if final_g is None else [final_g.reshape(1, -1).astype(F32)]
    return pl.pallas_call(
        _out_odd_kernel,
        grid=(n // tm,),
        in_specs=[
            pl.BlockSpec((tm, D_MODEL), lambda i: (i, 0)),
            pl.BlockSpec((tm, SSD_W), lambda i: (i, 0)),
            pl.BlockSpec((tm, SC_W), lambda i: (i, 0)),
            pl.BlockSpec(w1.shape, lambda i: (0, 0)),
            pl.BlockSpec(w2.shape, lambda i: (0, 0)),
        ] + [pl.BlockSpec((1, D_MODEL), lambda i: (0, 0)) for _ in extra],
        out_specs=pl.BlockSpec((tm, D_MODEL), lambda i: (i, 0)),
        out_shape=jax.ShapeDtypeStruct((n, D_MODEL), F32),
        compiler_params=_cparams(("parallel",)),
        name="out_proj_odd",
    )(x2d, oc, od, w1, w2, *extra)


def _final_norm_kernel(x_ref, g_ref, o_ref):
    o_ref[...] = _rms(x_ref[...], g_ref[...])


def _final_norm(x2d, g, tm):
    n = x2d.shape[0]
    return pl.pallas_call(
        _final_norm_kernel,
        grid=(n // tm,),
        in_specs=[pl.BlockSpec((tm, D_MODEL), lambda i: (i, 0)), pl.BlockSpec((1, D_MODEL), lambda i: (0, 0))],
        out_specs=pl.BlockSpec((tm, D_MODEL), lambda i: (i, 0)),
        out_shape=jax.ShapeDtypeStruct((n, D_MODEL), F32),
        compiler_params=_cparams(("parallel",)),
        name="final_norm",
    )(x2d, g)


def _tile(total, want):
    return want if total % want == 0 else total


def _even_layer(x2d, batch, seq, norm_g, w_in, w_uq, q_g, w_ukv, kv_g, conv_w, conv_b, ln_g, ln_b, w_out,
                tables):
    n = batch * seq
    low = Q_LORA + KV_LORA + QK_ROPE
    row = lambda v: v.reshape(1, -1).astype(F32)

    wa = jnp.pad(w_in[:, :low], ((0, 0), (0, 6 * LANES - low))).astype(BF16)
    w_rest = w_in[:, low:].astype(BF16)
    wq = w_uq.reshape(Q_LORA, MLA_HEADS, QK_NOPE + QK_ROPE)
    wqt = jnp.pad(wq, ((0, 0), (0, 0), (0, QK_PAD - QK_NOPE - QK_ROPE)))
    wqt = wqt.reshape(Q_LORA, MLA_HEADS * QK_PAD).T.astype(BF16)
    wkv = w_ukv.reshape(KV_LORA, MLA_HEADS, QK_NOPE + V_DIM)
    wk = jnp.pad(wkv[:, :, :QK_NOPE], ((0, 0), (0, 0), (0, QK_PAD - QK_NOPE)))
    wk = wk.reshape(KV_LORA, MLA_HEADS * QK_PAD).astype(BF16)
    wvt = jnp.pad(wkv[:, :, QK_NOPE:], ((0, 0), (0, 0), (0, VT_ROWS - V_DIM)))
    wvt = wvt.reshape(KV_LORA, MLA_HEADS * VT_ROWS).T.astype(BF16)
    cols = jnp.arange(MLA_HEADS * QK_PAD)
    place = ((cols[None, :] % QK_PAD) == (jnp.arange(LANES)[:, None] + QK_NOPE))
    place = (place & (jnp.arange(LANES)[:, None] < QK_ROPE)).astype(BF16)
    ones_col = ((jnp.arange(MLA_HEADS * VT_ROWS) % VT_ROWS) == V_DIM).astype(F32)[:, None]

    tm = _tile(seq, 512)
    u2 = _rms_matmul(x2d, row(norm_g), w_rest, _tile(n, 1024), 1024, BF16)
    qt, k, vt = _mla_prep(x2d, row(norm_g), wa, row(q_g), row(kv_g), wqt, wk, place, wvt, ones_col,
                          *tables, batch, seq, tm)
    ot = _attention(qt, k, vt, batch, seq, _tile(seq, 1024), tm)
    oa = _conformer(u2, conv_w.astype(F32), row(conv_b), row(ln_g), row(ln_b), batch, seq, tm)
    w_out = w_out.astype(BF16)
    return _out_even(x2d, ot, u2, oa, w_out[:MLA_W], w_out[MLA_W:], batch, seq, tm)


def _odd_layer(x2d, batch, seq, norm_g, w_in, conv_c_w, conv_c_b, dt_bias_f, dt_bias_b, a_log_f, a_log_b,
               d_skip, ssd_g, conv_d_w, w_out, final_g):
    n = batch * seq
    row = lambda v: v.reshape(1, -1).astype(F32)
    o_dt = SSD_W + XBC_W
    w_main = jnp.concatenate([w_in[:, :o_dt], w_in[:, o_dt + 2 * SSD_HEADS:]], axis=1).astype(BF16)
    w_dt = w_in[:, o_dt:o_dt + 2 * SSD_HEADS].reshape(D_MODEL, 2 * SSD_GROUPS, SSD_HPG)
    wdt_t = jnp.pad(w_dt, ((0, 0), (0, 0), (0, 8 - SSD_HPG))).reshape(D_MODEL, 64).T.astype(BF16)

    def per_head(vf, vb):
        v = jnp.concatenate([vf, vb]).reshape(2 * SSD_GROUPS, SSD_HPG)
        v = jnp.pad(v, ((0, 0), (0, 8 - SSD_HPG)))
        return jnp.broadcast_to(v[:, :, None], (2 * SSD_GROUPS, 8, SSD_CHUNK)).astype(F32)

    tm = _tile(seq, 512)
    uo = _rms_matmul(x2d, row(norm_g), w_main, _tile(n, 1024), 1024, BF16)
    dt_t = _dt_proj(x2d, row(norm_g), wdt_t, _tile(n, 1024)).reshape(2 * SSD_GROUPS, 8, n)
    xbc = _xbc_conv(uo, conv_c_w.astype(F32), row(conv_c_b), batch, seq, tm, 512, SSD_W // 512)
    y2 = _ssd(xbc, dt_t, per_head(dt_bias_f, dt_bias_b), per_head(a_log_f, a_log_b), batch, seq)
    dskip = row(jnp.repeat(d_skip, SSD_HEAD_DIM))
    oc = _ssd_gate(y2, xbc, uo, dskip, row(ssd_g), tm)
    od = _short_conv(uo, conv_d_w.astype(F32), batch, seq, tm, (SSD_W + XBC_W) // SC_W)
    w_out = w_out.astype(BF16)
    return _out_odd(x2d, oc, od, w_out[:SSD_W], w_out[SSD_W:], final_g, tm)


def _rope_tables(seq):
    half = QK_ROPE // 2
    inv = ROPE_THETA ** (-jnp.arange(half, dtype=F32) / half)
    ang = jnp.arange(seq, dtype=F32)[:, None] * inv[None, :]
    cos, sin = jnp.cos(ang), jnp.sin(ang)
    pad = ((0, 0), (0, LANES - QK_ROPE))
    cos2 = jnp.pad(jnp.concatenate([cos, cos], axis=1), pad)
    sin2 = jnp.pad(jnp.concatenate([-sin, sin], axis=1), pad)
    return cos2, sin2, cos.T, sin.T


def kernel(x, norm_e, w_in_e, w_uq, q_norm, w_ukv, kv_norm, conv_a_w, conv_a_b, ln_a_g, ln_a_b, w_out_e, norm_o, w_in_o, conv_c_w, conv_c_b, dt_bias_f, dt_bias_b, a_log_f, a_log_b, d_skip, ssd_norm, conv_d_w, w_out_o, final_norm):
    batch, seq, _ = x.shape
    depth = norm_e.shape[0] + norm_o.shape[0]
    tables = _rope_tables(seq)
    x2d = x.reshape(batch * seq, D_MODEL)
    for i in range(depth):
        j = i // 2
        if i % 2 == 0:
            x2d = _even_layer(x2d, batch, seq, norm_e[j], w_in_e[j], w_uq[j], q_norm[j], w_ukv[j], kv_norm[j],
                              conv_a_w[j], conv_a_b[j], ln_a_g[j], ln_a_b[j], w_out_e[j], tables)
        else:
            last = i == depth - 1
            x2d = _odd_layer(x2d, batch, seq, norm_o[j], w_in_o[j], conv_c_w[j], conv_c_b[j], dt_bias_f[j],
                             dt_bias_b[j], a_log_f[j], a_log_b[j], d_skip[j], ssd_norm[j], conv_d_w[j],
                             w_out_o[j], final_norm if last else None)
    if depth % 2 == 1:
        x2d = _final_norm(x2d, final_norm.reshape(1, -1), _tile(batch * seq, 512))
    return x2d.reshape(batch, seq, D_MODEL)
```

```python
import functools
import math

import jax
import jax.numpy as jnp
from jax import lax
from jax.experimental import pallas as pl
from jax.experimental.pallas import tpu as pltpu

F32 = jnp.float32
BF16 = jnp.bfloat16
EPS = 1e-6

D_MODEL = 1024
MLA_HEADS = 16
QK_NOPE = 64
QK_ROPE = 32
V_DIM = 64
Q_LORA = 384
KV_LORA = 256
MLA_W = MLA_HEADS * V_DIM
ROPE_THETA = 10000.0
CONV_W = 1024
CONV_K = 31
SSD_HEAD_DIM = 64
SSD_HEADS = 24
SSD_W = SSD_HEADS * SSD_HEAD_DIM
SSD_GROUPS = 4
SSD_HPG = SSD_HEADS // SSD_GROUPS
SSD_STATE = 128
SSD_CONV_K = 5
SSD_CHUNK = 128
XBC_W = SSD_W + 2 * SSD_GROUPS * SSD_STATE
SC_W = 512
SC_K = 3

LANES = 128
HALO = 16
QK_PAD = 128
VT_ROWS = 80
VMEM_LIMIT = 56 * 1024 * 1024


def _cparams(sem):
    return pltpu.CompilerParams(dimension_semantics=sem, vmem_limit_bytes=VMEM_LIMIT)


def _silu(v):
    return v * (1.0 / (1.0 + jnp.exp(-v)))


def _softplus(v):
    return jnp.maximum(v, 0.0) + jnp.log(1.0 + jnp.exp(-jnp.abs(v)))


def _rms(v, g):
    ms = jnp.mean(v * v, axis=-1, keepdims=True)
    return v * lax.rsqrt(ms + EPS) * g


def _rms_matmul_kernel(x_ref, g_ref, w_ref, o_ref, xn_ref):
    @pl.when(pl.program_id(1) == 0)
    def _():
        xn_ref[...] = _rms(x_ref[...], g_ref[...]).astype(xn_ref.dtype)

    o_ref[...] = jnp.dot(xn_ref[...], w_ref[...], preferred_element_type=F32).astype(o_ref.dtype)


def _rms_matmul(x, g, w, tm, tn, out_dtype):
    n, d = x.shape
    e = w.shape[1]
    return pl.pallas_call(
        _rms_matmul_kernel,
        grid=(n // tm, e // tn),
        in_specs=[
            pl.BlockSpec((tm, d), lambda i, j: (i, 0)),
            pl.BlockSpec((1, d), lambda i, j: (0, 0)),
            pl.BlockSpec((d, tn), lambda i, j: (0, j)),
        ],
        out_specs=pl.BlockSpec((tm, tn), lambda i, j: (i, j)),
        out_shape=jax.ShapeDtypeStruct((n, e), out_dtype),
        scratch_shapes=[pltpu.VMEM((tm, d), BF16)],
        compiler_params=_cparams(("parallel", "arbitrary")),
        name="rms_matmul",
    )(x, g, w)


def _mla_prep_kernel(x_ref, g_ref, wa_ref, qg_ref, kvg_ref, wqt_ref, wk_ref, place_ref, wvt_ref,
                     ones_ref, cos2_ref, sin2_ref, cost_ref, sint_ref,
                     qt_ref, k_ref, vt_ref, *, qscale):
    xn = _rms(x_ref[...], g_ref[...]).astype(BF16)
    a = jnp.dot(xn, wa_ref[...], preferred_element_type=F32)
    qn = _rms(a[:, :Q_LORA], qg_ref[...]).astype(BF16)
    kvn = _rms(a[:, Q_LORA:Q_LORA + KV_LORA], kvg_ref[...]).astype(BF16)

    kr = a[:, Q_LORA + KV_LORA:]
    lane = lax.broadcasted_iota(jnp.int32, kr.shape, 1)
    half = QK_ROPE // 2
    swapped = jnp.where(lane < half, pltpu.roll(kr, LANES - half, 1), pltpu.roll(kr, half, 1))
    kr = kr * cos2_ref[...] + swapped * sin2_ref[...]

    k = jnp.dot(kvn, wk_ref[...], preferred_element_type=F32)
    k = k + jnp.dot(kr.astype(BF16), place_ref[...], preferred_element_type=F32)
    k_ref[...] = k.astype(k_ref.dtype)

    nt = (((1,), (1,)), ((), ()))
    vt = lax.dot_general(wvt_ref[...], kvn, nt, preferred_element_type=F32) + ones_ref[...]
    vt_ref[...] = vt.astype(vt_ref.dtype)

    qt = lax.dot_general(wqt_ref[...], qn, nt, preferred_element_type=F32) * qscale
    qt_ref[...] = qt.astype(qt_ref.dtype)
    cos_t = cost_ref[...]
    sin_t = sint_ref[...]
    for h in range(MLA_HEADS):
        r0 = h * QK_PAD + QK_NOPE
        x1 = qt[r0:r0 + half]
        x2 = qt[r0 + half:r0 + 2 * half]
        qt_ref[r0:r0 + half, :] = (x1 * cos_t - x2 * sin_t).astype(qt_ref.dtype)
        qt_ref[r0 + half:r0 + 2 * half, :] = (x2 * cos_t + x1 * sin_t).astype(qt_ref.dtype)


def _mla_prep(x2d, g, wa, qg, kvg, wqt, wk, place, wvt, ones_col, cos2, sin2, cos_t, sin_t,
              batch, seq, tm):
    nblk = seq // tm
    qscale = (QK_NOPE + QK_ROPE) ** -0.5 * math.log2(math.e)
    full = lambda arr: pl.BlockSpec(arr.shape, lambda b, i: (0,) * arr.ndim)
    hq = MLA_HEADS * QK_PAD
    hv = MLA_HEADS * VT_ROWS
    return pl.pallas_call(
        functools.partial(_mla_prep_kernel, qscale=qscale),
        grid=(batch, nblk),
        in_specs=[
            pl.BlockSpec((tm, D_MODEL), lambda b, i: (b * nblk + i, 0)),
            full(g), full(wa), full(qg), full(kvg), full(wqt), full(wk), full(place), full(wvt),
            full(ones_col),
            pl.BlockSpec((tm, LANES), lambda b, i: (i, 0)),
            pl.BlockSpec((tm, LANES), lambda b, i: (i, 0)),
            pl.BlockSpec((QK_ROPE // 2, tm), lambda b, i: (0, i)),
            pl.BlockSpec((QK_ROPE // 2, tm), lambda b, i: (0, i)),
        ],
        out_specs=[
            pl.BlockSpec((None, hq, tm), lambda b, i: (b, 0, i)),
            pl.BlockSpec((None, tm, hq), lambda b, i: (b, i, 0)),
            pl.BlockSpec((None, None, hv, tm), lambda b, i: (b, i, 0, 0)),
        ],
        out_shape=[
            jax.ShapeDtypeStruct((batch, hq, seq), BF16),
            jax.ShapeDtypeStruct((batch, seq, hq), BF16),
            jax.ShapeDtypeStruct((batch, nblk, hv, tm), BF16),
        ],
        compiler_params=_cparams(("parallel", "parallel")),
        name="mla_prep",
    )(x2d, g, wa, qg, kvg, wqt, wk, place, wvt, ones_col, cos2, sin2, cos_t, sin_t)


def _attn_kernel(qt_ref, qn_ref, k_ref, vt_ref, o_ref, m_ref, acc_ref, s_ref, c_ref, *, tk, nk, unroll):
    def scores(q_ref, c, buf):
        k0 = pl.multiple_of(c * tk, tk)
        st = jnp.dot(k_ref[pl.ds(k0, tk), :], q_ref[...], preferred_element_type=F32)
        s_ref[buf] = st
        c_ref[buf] = jnp.max(st, axis=0, keepdims=True)

    def accumulate(c, buf):
        m_old = m_ref[...]
        m_new = jnp.maximum(m_old, c_ref[buf])
        p = jnp.exp2(s_ref[buf] - m_new).astype(BF16)
        pv = jnp.dot(vt_ref[c], p, preferred_element_type=F32)
        acc_ref[...] = acc_ref[...] * jnp.exp2(m_old - m_new) + pv
        m_ref[...] = m_new

    @pl.when(pl.program_id(2) == 0)
    def _():
        scores(qt_ref, 0, 0)

    m_ref[...] = jnp.full(m_ref.shape, -jnp.inf, F32)
    acc_ref[...] = jnp.zeros(acc_ref.shape, F32)

    def steps(c0, last):
        for u in range(unroll):
            if last and u == unroll - 1:
                scores(qn_ref, 0, 0)
            else:
                scores(qt_ref, c0 + u + 1, (u + 1) % 2)
            accumulate(c0 + u, u % 2)

    def body(i, carry):
        steps(i * unroll, False)
        return carry

    lax.fori_loop(0, nk // unroll - 1, body, 0)
    steps(nk - unroll, True)
    acc = acc_ref[...]
    o_ref[...] = (acc[:V_DIM] * (1.0 / acc[V_DIM:V_DIM + 1])).astype(o_ref.dtype)


def _attention(qt, k, vt, batch, seq, tq, tk):
    nk = seq // tk
    nq = seq // tq
    unroll = 4 if nk % 4 == 0 else 2
    assert nk % unroll == 0, "the key-chunk pipeline alternates between two score buffers"
    return pl.pallas_call(
        functools.partial(_attn_kernel, tk=tk, nk=nk, unroll=unroll),
        grid=(batch, MLA_HEADS, nq),
        in_specs=[
            pl.BlockSpec((None, QK_PAD, tq), lambda b, h, i: (b, h, i)),
            pl.BlockSpec((None, QK_PAD, tq), lambda b, h, i: (b, h, jnp.minimum(i + 1, nq - 1))),
            pl.BlockSpec((None, seq, QK_PAD), lambda b, h, i: (b, 0, h)),
            pl.BlockSpec((None, nk, VT_ROWS, tk), lambda b, h, i: (b, 0, h, 0)),
        ],
        out_specs=pl.BlockSpec((None, V_DIM, tq), lambda b, h, i: (b, h, i)),
        out_shape=jax.ShapeDtypeStruct((batch, MLA_W, seq), BF16),
        scratch_shapes=[pltpu.VMEM((1, tq), F32), pltpu.VMEM((VT_ROWS, tq), F32),
                        pltpu.VMEM((2, tk, tq), F32), pltpu.VMEM((2, 1, tq), F32)],
        compiler_params=_cparams(("parallel", "parallel", "arbitrary")),
        name="mla_attention",
    )(qt, qt, k, vt)


def _fill_ext(ext_ref, prev, cur, nxt, i, n_i, ts):
    ext_ref[0:HALO, :] = jnp.where(i > 0, prev, 0.0)
    ext_ref[HALO:HALO + ts, :] = cur
    ext_ref[HALO + ts:2 * HALO + ts, :] = jnp.where(i < n_i - 1, nxt, 0.0)


def _dwconv(ext_ref, w_ref, k_taps, ts, width, emit, rows=128):
    sub = 8
    off = HALO - k_taps // 2
    lo = (off // sub) * sub
    win_rows = ((off - lo + k_taps - 1) // sub + 1) * sub + rows

    def body(rc, carry):
        r0 = pl.multiple_of(rc * rows, rows)
        for cb in range(width // LANES):
            sl = slice(cb * LANES, (cb + 1) * LANES)
            win = ext_ref[pl.ds(r0 + lo, win_rows), sl]
            acc = jnp.zeros((rows, LANES), F32)
            for phase in range(sub):
                taps = [k for k in range(k_taps) if (off - lo + k) % sub == phase]
                if not taps:
                    continue
                shifted = win if phase == 0 else pltpu.roll(win, win_rows - phase, 0)
                for k in taps:
                    q = (off - lo + k) // sub * sub
                    acc = acc + w_ref[k:k + 1, sl] * shifted[q:q + rows]
            emit(r0, sl, acc)
        return carry

    lax.fori_loop(0, ts // rows, body, 0)


def _halo_specs(ts, width, col, nblk, batch):
    per = ts // HALO
    last = batch * nblk * per - 1

    def prev_map(b, i, *_):
        return (jnp.maximum((b * nblk + i) * per - 1, 0), col(*_))

    def cur_map(b, i, *_):
        return (b * nblk + i, col(*_))

    def next_map(b, i, *_):
        return (jnp.minimum((b * nblk + i + 1) * per, last), col(*_))

    return [pl.BlockSpec((HALO, width), prev_map), pl.BlockSpec((ts, width), cur_map),
            pl.BlockSpec((HALO, width), next_map)]


def _conformer_kernel(ap_ref, a_ref, an_ref, gp_ref, g_ref, gn_ref, z_ref, w_ref, b_ref, lng_ref,
                      lnb_ref, o_ref, ext_ref, conv_ref, *, ts):
    i = pl.program_id(1)
    n_i = pl.num_programs(1)

    def glu(a, g):
        return a.astype(F32) * (1.0 / (1.0 + jnp.exp(-g.astype(F32))))

    _fill_ext(ext_ref, glu(ap_ref[...], gp_ref[...]), glu(a_ref[...], g_ref[...]),
              glu(an_ref[...], gn_ref[...]), i, n_i, ts)

    def emit(r0, sl, acc):
        conv_ref[pl.ds(r0, acc.shape[0]), sl] = acc + b_ref[:, sl]

    _dwconv(ext_ref, w_ref, CONV_K, ts, CONV_W, emit)

    rows = 128

    def ln_body(rc, carry):
        r0 = pl.multiple_of(rc * rows, rows)
        v = conv_ref[pl.ds(r0, rows), :]
        vc = v - jnp.mean(v, axis=-1, keepdims=True)
        var = jnp.mean(vc * vc, axis=-1, keepdims=True)
        y = vc * lax.rsqrt(var + EPS) * lng_ref[...] + lnb_ref[...]
        y = _silu(y) * _silu(z_ref[pl.ds(r0, rows), :].astype(F32))
        o_ref[pl.ds(r0, rows), :] = y.astype(o_ref.dtype)
        return carry

    lax.fori_loop(0, ts // rows, ln_body, 0)


def _conformer(u2, w, b, lng, lnb, batch, seq, ts):
    n = batch * seq
    nblk = seq // ts
    a_specs = _halo_specs(ts, CONV_W, lambda: 1, nblk, batch)
    g_specs = _halo_specs(ts, CONV_W, lambda: 2, nblk, batch)
    small = lambda arr: pl.BlockSpec(arr.shape, lambda b_, i: (0, 0))
    return pl.pallas_call(
        functools.partial(_conformer_kernel, ts=ts),
        grid=(batch, nblk),
        in_specs=a_specs + g_specs + [
            pl.BlockSpec((ts, CONV_W), lambda b_, i: (b_ * nblk + i, 3)),
            small(w), small(b), small(lng), small(lnb)],
        out_specs=pl.BlockSpec((ts, CONV_W), lambda b_, i: (b_ * nblk + i, 0)),
        out_shape=jax.ShapeDtypeStruct((n, CONV_W), BF16),
        scratch_shapes=[pltpu.VMEM((ts + 2 * HALO, CONV_W), F32), pltpu.VMEM((ts, CONV_W), F32)],
        compiler_params=_cparams(("parallel", "parallel")),
        name="conformer_conv",
    )(u2, u2, u2, u2, u2, u2, u2, w, b, lng, lnb)


def _out_even_kernel(x_ref, ot_ref, z_ref, oa_ref, w1_ref, w2_ref, o_ref):
    ob = ot_ref[...].astype(F32).T * _silu(z_ref[...].astype(F32))
    acc = jnp.dot(ob.astype(BF16), w1_ref[...], preferred_element_type=F32)
    acc = acc + jnp.dot(oa_ref[...], w2_ref[...], preferred_element_type=F32)
    o_ref[...] = x_ref[...] + acc


def _out_even(x2d, ot, u2, oa, w1, w2, batch, seq, tm):
    n = batch * seq
    nblk = seq // tm
    return pl.pallas_call(
        _out_even_kernel,
        grid=(batch, nblk),
        in_specs=[
            pl.BlockSpec((tm, D_MODEL), lambda b, i: (b * nblk + i, 0)),
            pl.BlockSpec((None, MLA_W, tm), lambda b, i: (b, 0, i)),
            pl.BlockSpec((tm, MLA_W), lambda b, i: (b * nblk + i, 0)),
            pl.BlockSpec((tm, CONV_W), lambda b, i: (b * nblk + i, 0)),
            pl.BlockSpec(w1.shape, lambda b, i: (0, 0)),
            pl.BlockSpec(w2.shape, lambda b, i: (0, 0)),
        ],
        out_specs=pl.BlockSpec((tm, D_MODEL), lambda b, i: (b * nblk + i, 0)),
        out_shape=jax.ShapeDtypeStruct((n, D_MODEL), F32),
        compiler_params=_cparams(("parallel", "parallel")),
        name="out_proj_even",
    )(x2d, ot, u2, oa, w1, w2)


def _dt_kernel(x_ref, g_ref, w_ref, o_ref):
    xn = _rms(x_ref[...], g_ref[...]).astype(BF16)
    o_ref[...] = lax.dot_general(w_ref[...], xn, (((1,), (1,)), ((), ())), preferred_element_type=F32)


def _dt_proj(x2d, g, wdt_t, tm):
    n = x2d.shape[0]
    rows = wdt_t.shape[0]
    return pl.pallas_call(
        _dt_kernel,
        grid=(n // tm,),
        in_specs=[pl.BlockSpec((tm, D_MODEL), lambda i: (i, 0)),
                  pl.BlockSpec((1, D_MODEL), lambda i: (0, 0)),
                  pl.BlockSpec(wdt_t.shape, lambda i: (0, 0))],
        out_specs=pl.BlockSpec((rows, tm), lambda i: (0, i)),
        out_shape=jax.ShapeDtypeStruct((rows, n), F32),
        compiler_params=_cparams(("parallel",)),
        name="dt_proj",
    )(x2d, g, wdt_t)


def _xbc_conv_kernel(p_ref, c_ref, n_ref, w_ref, b_ref, o_ref, ext_ref, *, ts, width):
    i = pl.program_id(1)
    _fill_ext(ext_ref, p_ref[...].astype(F32), c_ref[...].astype(F32), n_ref[...].astype(F32),
              i, pl.num_programs(1), ts)

    def emit(r0, sl, acc):
        o_ref[pl.ds(r0, acc.shape[0]), sl] = _silu(acc + b_ref[:, sl]).astype(o_ref.dtype)

    _dwconv(ext_ref, w_ref, SSD_CONV_K, ts, width, emit)


def _xbc_conv(uo, w, b, batch, seq, ts, width, col0):
    n = batch * seq
    nblk = seq // ts
    ncol = XBC_W // width
    specs = _halo_specs(ts, width, lambda j: col0 + j, nblk, batch)
    return pl.pallas_call(
        functools.partial(_xbc_conv_kernel, ts=ts, width=width),
        grid=(batch, nblk, ncol),
        in_specs=specs + [pl.BlockSpec((SSD_CONV_K, width), lambda b_, i, j: (0, j)),
                          pl.BlockSpec((1, width), lambda b_, i, j: (0, j))],
        out_specs=pl.BlockSpec((ts, width), lambda b_, i, j: (b_ * nblk + i, j)),
        out_shape=jax.ShapeDtypeStruct((n, XBC_W), BF16),
        scratch_shapes=[pltpu.VMEM((ts + 2 * HALO, width), F32)],
        compiler_params=_cparams(("parallel", "parallel", "parallel")),
        name="xbc_conv",
    )(uo, uo, uo, w, b)


def _ssd_kernel(xbc_ref, dt_ref, bias_ref, alog_ref, y_ref, state_ref):
    d = pl.program_id(1)
    c = pl.program_id(2)
    L = SSD_CHUNK
    G = SSD_GROUPS
    pairs = SSD_HPG // 2
    gw = SSD_HPG * SSD_HEAD_DIM

    @pl.when(c == 0)
    def _():
        state_ref[...] = jnp.zeros(state_ref.shape, F32)

    row = lax.broadcasted_iota(jnp.int32, (L, L), 0)
    col = lax.broadcasted_iota(jnp.int32, (L, L), 1)
    lane_lo = col < SSD_HEAD_DIM
    ahead = (row - col) * (1 - 2 * d)
    mask = ahead >= 0

    stack_rows = lambda ref: jnp.concatenate([ref[g] for g in range(G)], axis=0)
    dtv = _softplus(stack_rows(dt_ref) + stack_rows(bias_ref))
    da = dtv * (-jnp.exp(stack_rows(alog_ref)))
    tri = jnp.where(ahead <= 0, 1.0, 0.0)
    cs_t = jnp.dot(da, tri, preferred_element_type=F32, precision=lax.Precision.HIGHEST)
    tot = jnp.sum(da, axis=1, keepdims=True)
    etot = jnp.exp(jnp.broadcast_to(tot, cs_t.shape))
    w_t = jnp.exp(tot - cs_t) * dtv
    srow_t = cs_t - jnp.log(dtv)
    cs_cols = jnp.concatenate([cs_t, jnp.zeros((L - 8 * G, L), F32)], axis=0).T

    for g in range(G):
        bm = xbc_ref[:, SSD_W + g * SSD_STATE:SSD_W + (g + 1) * SSD_STATE]
        cm = xbc_ref[:, SSD_W + (G + g) * SSD_STATE:SSD_W + (G + g + 1) * SSD_STATE]
        cb = lax.dot_general(cm, bm, (((1,), (1,)), ((), ())), preferred_element_type=F32)
        bt = bm.astype(F32).T
        cmf = cm.astype(F32)
        for j in range(pairs):
            lanes = slice(g * gw + j * LANES, g * gw + (j + 1) * LANES)
            xs_pair = xbc_ref[:, lanes]
            st_pair = state_ref[g * pairs + j]
            rhs = jnp.concatenate([xs_pair, st_pair.astype(BF16)], axis=0)
            ys = []
            news = []
            for r in (8 * g + 2 * j, 8 * g + 2 * j + 1):
                cs_col = jnp.broadcast_to(cs_cols[:, r:r + 1], (L, L))
                m = jnp.where(mask, jnp.exp(cs_col - srow_t[r:r + 1, :]), 0.0) * cb
                cs_scaled = cmf * jnp.exp(cs_col)
                lhs = jnp.concatenate([m.astype(BF16), cs_scaled.astype(BF16)], axis=1)
                ys.append(jnp.dot(lhs, rhs, preferred_element_type=F32))
                btw = (bt * w_t[r:r + 1, :]).astype(BF16)
                news.append(jnp.dot(btw, xs_pair, preferred_element_type=F32))
            y_ref[:, lanes] = jnp.where(lane_lo, ys[0], ys[1]).astype(y_ref.dtype)
            r0 = 8 * g + 2 * j
            decay = jnp.where(lane_lo[:1], etot[r0:r0 + 1, :], etot[r0 + 1:r0 + 2, :])
            state_ref[g * pairs + j] = st_pair * decay + jnp.where(lane_lo, news[0], news[1])


def _ssd(xbc, dt_t, bias, alog, batch, seq):
    n = batch * seq
    nc = seq // SSD_CHUNK

    def tpos(b, d, c):
        return b * nc + jnp.where(d == 0, c, nc - 1 - c)

    head_rows = (None, SSD_GROUPS, 8, SSD_CHUNK)
    return pl.pallas_call(
        _ssd_kernel,
        grid=(batch, 2, nc),
        in_specs=[
            pl.BlockSpec((SSD_CHUNK, XBC_W), lambda b, d, c: (tpos(b, d, c), 0)),
            pl.BlockSpec(head_rows, lambda b, d, c: (d, 0, 0, tpos(b, d, c))),
            pl.BlockSpec(head_rows, lambda b, d, c: (d, 0, 0, 0)),
            pl.BlockSpec(head_rows, lambda b, d, c: (d, 0, 0, 0)),
        ],
        out_specs=pl.BlockSpec((None, SSD_CHUNK, SSD_W), lambda b, d, c: (d, tpos(b, d, c), 0)),
        out_shape=jax.ShapeDtypeStruct((2, n, SSD_W), BF16),
        scratch_shapes=[pltpu.VMEM((SSD_GROUPS * SSD_HPG // 2, SSD_STATE, LANES), F32)],
        compiler_params=_cparams(("parallel", "arbitrary", "arbitrary")),
        name="ssd_scan",
    )(xbc, dt_t, bias, alog)


def _ssd_gate_kernel(y_ref, xs_ref, z_ref, dskip_ref, g_ref, o_ref):
    gw = SSD_HPG * SSD_HEAD_DIM
    y = y_ref[0].astype(F32) + y_ref[1].astype(F32) + dskip_ref[...] * xs_ref[...].astype(F32)
    y = y * _silu(z_ref[...].astype(F32))
    for g in range(SSD_GROUPS):
        sl = slice(g * gw, (g + 1) * gw)
        o_ref[:, sl] = _rms(y[:, sl], g_ref[:, sl]).astype(o_ref.dtype)


def _ssd_gate(y2, xbc, uo, dskip, g, tm):
    n = xbc.shape[0]
    return pl.pallas_call(
        _ssd_gate_kernel,
        grid=(n // tm,),
        in_specs=[
            pl.BlockSpec((2, tm, SSD_W), lambda i: (0, i, 0)),
            pl.BlockSpec((tm, SSD_W), lambda i: (i, 0)),
            pl.BlockSpec((tm, SSD_W), lambda i: (i, 0)),
            pl.BlockSpec((1, SSD_W), lambda i: (0, 0)),
            pl.BlockSpec((1, SSD_W), lambda i: (0, 0)),
        ],
        out_specs=pl.BlockSpec((tm, SSD_W), lambda i: (i, 0)),
        out_shape=jax.ShapeDtypeStruct((n, SSD_W), BF16),
        compiler_params=_cparams(("parallel",)),
        name="ssd_gate_norm",
    )(y2, xbc, uo, dskip, g)


def _short_conv_kernel(gcp_ref, gc_ref, gcn_ref, hp_ref, h_ref, hn_ref, gb_ref, z_ref, w_ref,
                       o_ref, ext_ref, *, ts):
    i = pl.program_id(1)
    mul = lambda a, b: a.astype(F32) * b.astype(F32)
    _fill_ext(ext_ref, mul(gcp_ref[...], hp_ref[...]), mul(gc_ref[...], h_ref[...]),
              mul(gcn_ref[...], hn_ref[...]), i, pl.num_programs(1), ts)

    def emit(r0, sl, acc):
        rs = pl.ds(r0, acc.shape[0])
        y = gb_ref[rs, sl].astype(F32) * acc * _silu(z_ref[rs, sl].astype(F32))
        o_ref[rs, sl] = y.astype(o_ref.dtype)

    _dwconv(ext_ref, w_ref, SC_K, ts, SC_W, emit)


def _short_conv(uo, w, batch, seq, ts, col0):
    n = batch * seq
    nblk = seq // ts
    gc_specs = _halo_specs(ts, SC_W, lambda: col0 + 1, nblk, batch)
    h_specs = _halo_specs(ts, SC_W, lambda: col0 + 2, nblk, batch)
    return pl.pallas_call(
        functools.partial(_short_conv_kernel, ts=ts),
        grid=(batch, nblk),
        in_specs=gc_specs + h_specs + [
            pl.BlockSpec((ts, SC_W), lambda b, i: (b * nblk + i, col0)),
            pl.BlockSpec((ts, SC_W), lambda b, i: (b * nblk + i, col0 + 3)),
            pl.BlockSpec(w.shape, lambda b, i: (0, 0))],
        out_specs=pl.BlockSpec((ts, SC_W), lambda b, i: (b * nblk + i, 0)),
        out_shape=jax.ShapeDtypeStruct((n, SC_W), BF16),
        scratch_shapes=[pltpu.VMEM((ts + 2 * HALO, SC_W), F32)],
        compiler_params=_cparams(("parallel", "parallel")),
        name="short_conv",
    )(uo, uo, uo, uo, uo, uo, uo, uo, w)


def _out_odd_kernel(x_ref, oc_ref, od_ref, w1_ref, w2_ref, *rest):
    o_ref = rest[-1]
    acc = jnp.dot(oc_ref[...], w1_ref[...], preferred_element_type=F32)
    acc = acc + jnp.dot(od_ref[...], w2_ref[...], preferred_element_type=F32)
    y = x_ref[...] + acc
    if len(rest) == 2:
        y = _rms(y, rest[0][...])
    o_ref[...] = y


def _out_odd(x2d, oc, od, w1, w2, final_g, tm):
    n = x2d.shape[0]
    extra = [] if final_g is None else [final_g.reshape(1, -1).astype(F32)]
    return pl.pallas_call(
        _out_odd_kernel,
        grid=(n // tm,),
        in_specs=[
            pl.BlockSpec((tm, D_MODEL), lambda i: (i, 0)),
            pl.BlockSpec((tm, SSD_W), lambda i: (i, 0)),
            pl.BlockSpec((tm, SC_W), lambda i: (i, 0)),
            pl.BlockSpec(w1.shape, lambda i: (0, 0)),
            pl.BlockSpec(w2.shape, lambda i: (0, 0)),
        ] + [pl.BlockSpec((1, D_MODEL), lambda i: (0, 0)) for _ in extra],
        out_specs=pl.BlockSpec((tm, D_MODEL), lambda i: (i, 0)),
        out_shape=jax.ShapeDtypeStruct((n, D_MODEL), F32),
        compiler_params=_cparams(("parallel",)),
        name="out_proj_odd",
    )(x2d, oc, od, w1, w2, *extra)


def _final_norm_kernel(x_ref, g_ref, o_ref):
    o_ref[...] = _rms(x_ref[...], g_ref[...])


def _final_norm(x2d, g, tm):
    n = x2d.shape[0]
    return pl.pallas_call(
        _final_norm_kernel,
        grid=(n // tm,),
        in_specs=[pl.BlockSpec((tm, D_MODEL), lambda i: (i, 0)), pl.BlockSpec((1, D_MODEL), lambda i: (0, 0))],
        out_specs=pl.BlockSpec((tm, D_MODEL), lambda i: (i, 0)),
        out_shape=jax.ShapeDtypeStruct((n, D_MODEL), F32),
        compiler_params=_cparams(("parallel",)),
        name="final_norm",
    )(x2d, g)


def _tile(total, want):
    return want if total % want == 0 else total


def _even_layer(x2d, batch, seq, norm_g, w_in, w_uq, q_g, w_ukv, kv_g, conv_w, conv_b, ln_g, ln_b, w_out,
                tables):
    n = batch * seq
    low = Q_LORA + KV_LORA + QK_ROPE
    row = lambda v: v.reshape(1, -1).astype(F32)

    wa = jnp.pad(w_in[:, :low], ((0, 0), (0, 6 * LANES - low))).astype(BF16)
    w_rest = w_in[:, low:].astype(BF16)
    wq = w_uq.reshape(Q_LORA, MLA_HEADS, QK_NOPE + QK_ROPE)
    wqt = jnp.pad(wq, ((0, 0), (0, 0), (0, QK_PAD - QK_NOPE - QK_ROPE)))
    wqt = wqt.reshape(Q_LORA, MLA_HEADS * QK_PAD).T.astype(BF16)
    wkv = w_ukv.reshape(KV_LORA, MLA_HEADS, QK_NOPE + V_DIM)
    wk = jnp.pad(wkv[:, :, :QK_NOPE], ((0, 0), (0, 0), (0, QK_PAD - QK_NOPE)))
    wk = wk.reshape(KV_LORA, MLA_HEADS * QK_PAD).astype(BF16)
    wvt = jnp.pad(wkv[:, :, QK_NOPE:], ((0, 0), (0, 0), (0, VT_ROWS - V_DIM)))
    wvt = wvt.reshape(KV_LORA, MLA_HEADS * VT_ROWS).T.astype(BF16)
    cols = jnp.arange(MLA_HEADS * QK_PAD)
    place = ((cols[None, :] % QK_PAD) == (jnp.arange(LANES)[:, None] + QK_NOPE))
    place = (place & (jnp.arange(LANES)[:, None] < QK_ROPE)).astype(BF16)
    ones_col = ((jnp.arange(MLA_HEADS * VT_ROWS) % VT_ROWS) == V_DIM).astype(F32)[:, None]

    tm = _tile(seq, 512)
    u2 = _rms_matmul(x2d, row(norm_g), w_rest, _tile(n, 1024), 1024, BF16)
    qt, k, vt = _mla_prep(x2d, row(norm_g), wa, row(q_g), row(kv_g), wqt, wk, place, wvt, ones_col,
                          *tables, batch, seq, tm)
    ot = _attention(qt, k, vt, batch, seq, _tile(seq, 1024), tm)
    oa = _conformer(u2, conv_w.astype(F32), row(conv_b), row(ln_g), row(ln_b), batch, seq, tm)
    w_out = w_out.astype(BF16)
    return _out_even(x2d, ot, u2, oa, w_out[:MLA_W], w_out[MLA_W:], batch, seq, tm)


def _odd_layer(x2d, batch, seq, norm_g, w_in, conv_c_w, conv_c_b, dt_bias_f, dt_bias_b, a_log_f, a_log_b,
               d_skip, ssd_g, conv_d_w, w_out, final_g):
    n = batch * seq
    row = lambda v: v.reshape(1, -1).astype(F32)
    o_dt = SSD_W + XBC_W
    w_main = jnp.concatenate([w_in[:, :o_dt], w_in[:, o_dt + 2 * SSD_HEADS:]], axis=1).astype(BF16)
    w_dt = w_in[:, o_dt:o_dt + 2 * SSD_HEADS].reshape(D_MODEL, 2 * SSD_GROUPS, SSD_HPG)
    wdt_t = jnp.pad(w_dt, ((0, 0), (0, 0), (0, 8 - SSD_HPG))).reshape(D_MODEL, 64).T.astype(BF16)

    def per_head(vf, vb):
        v = jnp.concatenate([vf, vb]).reshape(2, SSD_GROUPS, SSD_HPG)
        v = jnp.pad(v, ((0, 0), (0, 0), (0, 8 - SSD_HPG)))
        return jnp.broadcast_to(v[..., None], (2, SSD_GROUPS, 8, SSD_CHUNK)).astype(F32)

    tm = _tile(seq, 512)
    uo = _rms_matmul(x2d, row(norm_g), w_main, _tile(n, 1024), 1024, BF16)
    dt_t = _dt_proj(x2d, row(norm_g), wdt_t, _tile(n, 1024)).reshape(2, SSD_GROUPS, 8, n)
    xbc = _xbc_conv(uo, conv_c_w.astype(F32), row(conv_c_b), batch, seq, tm, 512, SSD_W // 512)
    y2 = _ssd(xbc, dt_t, per_head(dt_bias_f, dt_bias_b), per_head(a_log_f, a_log_b), batch, seq)
    dskip = row(jnp.repeat(d_skip, SSD_HEAD_DIM))
    oc = _ssd_gate(y2, xbc, uo, dskip, row(ssd_g), tm)
    od = _short_conv(uo, conv_d_w.astype(F32), batch, seq, tm, (SSD_W + XBC_W) // SC_W)
    w_out = w_out.astype(BF16)
    return _out_odd(x2d, oc, od, w_out[:SSD_W], w_out[SSD_W:], final_g, tm)


def _rope_tables(seq):
    half = QK_ROPE // 2
    inv = ROPE_THETA ** (-jnp.arange(half, dtype=F32) / half)
    ang = jnp.arange(seq, dtype=F32)[:, None] * inv[None, :]
    cos, sin = jnp.cos(ang), jnp.sin(ang)
    pad = ((0, 0), (0, LANES - QK_ROPE))
    cos2 = jnp.pad(jnp.concatenate([cos, cos], axis=1), pad)
    sin2 = jnp.pad(jnp.concatenate([-sin, sin], axis=1), pad)
    return cos2, sin2, cos.T, sin.T


def kernel(x, norm_e, w_in_e, w_uq, q_norm, w_ukv, kv_norm, conv_a_w, conv_a_b, ln_a_g, ln_a_b, w_out_e, norm_o, w_in_o, conv_c_w, conv_c_b, dt_bias_f, dt_bias_b, a_log_f, a_log_b, d_skip, ssd_norm, conv_d_w, w_out_o, final_norm):
    batch, seq, _ = x.shape
    depth = norm_e.shape[0] + norm_o.shape[0]
    tables = _rope_tables(seq)
    x2d = x.reshape(batch * seq, D_MODEL)
    for i in range(depth):
        j = i // 2
        if i % 2 == 0:
            x2d = _even_layer(x2d, batch, seq, norm_e[j], w_in_e[j], w_uq[j], q_norm[j], w_ukv[j], kv_norm[j],
                              conv_a_w[j], conv_a_b[j], ln_a_g[j], ln_a_b[j], w_out_e[j], tables)
        else:
            last = i == depth - 1
            x2d = _odd_layer(x2d, batch, seq, norm_o[j], w_in_o[j], conv_c_w[j], conv_c_b[j], dt_bias_f[j],
                             dt_bias_b[j], a_log_f[j], a_log_b[j], d_skip[j], ssd_norm[j], conv_d_w[j],
                             w_out_o[j], final_norm if last else None)
    if depth % 2 == 1:
        x2d = _final_norm(x2d, final_norm.reshape(1, -1), _tile(batch * seq, 512))
    return x2d.reshape(batch, seq, D_MODEL)
```

```python
import functools
import math

import jax
import jax.numpy as jnp
from jax import lax
from jax.experimental import pallas as pl
from jax.experimental.pallas import tpu as pltpu

F32 = jnp.float32
BF16 = jnp.bfloat16
EPS = 1e-6

D_MODEL = 1024
MLA_HEADS = 16
QK_NOPE = 64
QK_ROPE = 32
V_DIM = 64
Q_LORA = 384
KV_LORA = 256
MLA_W = MLA_HEADS * V_DIM
ROPE_THETA = 10000.0
CONV_W = 1024
CONV_K = 31
SSD_HEAD_DIM = 64
SSD_HEADS = 24
SSD_W = SSD_HEADS * SSD_HEAD_DIM
SSD_GROUPS = 4
SSD_HPG = SSD_HEADS // SSD_GROUPS
SSD_STATE = 128
SSD_CONV_K = 5
SSD_CHUNK = 128
XBC_W = SSD_W + 2 * SSD_GROUPS * SSD_STATE
SC_W = 512
SC_K = 3

LANES = 128
HALO = 16
QK_PAD = 128
VT_ROWS = 80
ATTN_SAFE_MAX = 1e30
VMEM_LIMIT = 56 * 1024 * 1024

TOKEN_TILE = 512
QUERY_TILE = 1024
PROJ_ROWS = 1024
PROJ_COLS = 2048
OUT_ROWS = 1024


def _cparams(sem):
    return pltpu.CompilerParams(dimension_semantics=sem, vmem_limit_bytes=VMEM_LIMIT)


def _silu(v):
    return v * (1.0 / (1.0 + jnp.exp(-v)))


def _softplus(v):
    return jnp.maximum(v, 0.0) + jnp.log(1.0 + jnp.exp(-jnp.abs(v)))


def _rms(v, g):
    ms = jnp.mean(v * v, axis=-1, keepdims=True)
    return v * lax.rsqrt(ms + EPS) * g


def _rms_matmul_kernel(x_ref, g_ref, w_ref, o_ref, xn_ref):
    @pl.when(pl.program_id(1) == 0)
    def _():
        xn_ref[...] = _rms(x_ref[...], g_ref[...]).astype(xn_ref.dtype)

    o_ref[...] = jnp.dot(xn_ref[...], w_ref[...], preferred_element_type=F32).astype(o_ref.dtype)


def _rms_matmul(x, g, w, tm, tn, out_dtype):
    n, d = x.shape
    e = w.shape[1]
    return pl.pallas_call(
        _rms_matmul_kernel,
        grid=(n // tm, e // tn),
        in_specs=[
            pl.BlockSpec((tm, d), lambda i, j: (i, 0)),
            pl.BlockSpec((1, d), lambda i, j: (0, 0)),
            pl.BlockSpec((d, tn), lambda i, j: (0, j)),
        ],
        out_specs=pl.BlockSpec((tm, tn), lambda i, j: (i, j)),
        out_shape=jax.ShapeDtypeStruct((n, e), out_dtype),
        scratch_shapes=[pltpu.VMEM((tm, d), BF16)],
        compiler_params=_cparams(("parallel", "arbitrary")),
        name="rms_matmul",
    )(x, g, w)


def _mla_prep_kernel(x_ref, g_ref, wa_ref, qg_ref, kvg_ref, wqt_ref, wk_ref, place_ref, wvt_ref,
                     ones_ref, cos2_ref, sin2_ref, cost_ref, sint_ref,
                     qt_ref, k_ref, vt_ref, *, qscale):
    xn = _rms(x_ref[...], g_ref[...]).astype(BF16)
    a = jnp.dot(xn, wa_ref[...], preferred_element_type=F32)
    qn = _rms(a[:, :Q_LORA], qg_ref[...]).astype(BF16)
    kvn = _rms(a[:, Q_LORA:Q_LORA + KV_LORA], kvg_ref[...]).astype(BF16)

    kr = a[:, Q_LORA + KV_LORA:]
    lane = lax.broadcasted_iota(jnp.int32, kr.shape, 1)
    half = QK_ROPE // 2
    swapped = jnp.where(lane < half, pltpu.roll(kr, LANES - half, 1), pltpu.roll(kr, half, 1))
    kr = kr * cos2_ref[...] + swapped * sin2_ref[...]

    k = jnp.dot(kvn, wk_ref[...], preferred_element_type=F32)
    k = k + jnp.dot(kr.astype(BF16), place_ref[...], preferred_element_type=F32)
    k_ref[...] = k.astype(k_ref.dtype)

    nt = (((1,), (1,)), ((), ()))
    vt = lax.dot_general(wvt_ref[...], kvn, nt, preferred_element_type=F32) + ones_ref[...]
    vt_ref[...] = vt.astype(vt_ref.dtype)

    qt = lax.dot_general(wqt_ref[...], qn, nt, preferred_element_type=F32) * qscale
    qt_ref[...] = qt.astype(qt_ref.dtype)
    cos_t = cost_ref[...]
    sin_t = sint_ref[...]
    for h in range(MLA_HEADS):
        r0 = h * QK_PAD + QK_NOPE
        x1 = qt[r0:r0 + half]
        x2 = qt[r0 + half:r0 + 2 * half]
        qt_ref[r0:r0 + half, :] = (x1 * cos_t - x2 * sin_t).astype(qt_ref.dtype)
        qt_ref[r0 + half:r0 + 2 * half, :] = (x2 * cos_t + x1 * sin_t).astype(qt_ref.dtype)


def _mla_prep(x2d, g, wa, qg, kvg, wqt, wk, place, wvt, ones_col, cos2, sin2, cos_t, sin_t,
              batch, seq, tm):
    nblk = seq // tm
    qscale = (QK_NOPE + QK_ROPE) ** -0.5 * math.log2(math.e)
    full = lambda arr: pl.BlockSpec(arr.shape, lambda b, i: (0,) * arr.ndim)
    hq = MLA_HEADS * QK_PAD
    hv = MLA_HEADS * VT_ROWS
    return pl.pallas_call(
        functools.partial(_mla_prep_kernel, qscale=qscale),
        grid=(batch, nblk),
        in_specs=[
            pl.BlockSpec((tm, D_MODEL), lambda b, i: (b * nblk + i, 0)),
            full(g), full(wa), full(qg), full(kvg), full(wqt), full(wk), full(place), full(wvt),
            full(ones_col),
            pl.BlockSpec((tm, LANES), lambda b, i: (i, 0)),
            pl.BlockSpec((tm, LANES), lambda b, i: (i, 0)),
            pl.BlockSpec((QK_ROPE // 2, tm), lambda b, i: (0, i)),
            pl.BlockSpec((QK_ROPE // 2, tm), lambda b, i: (0, i)),
        ],
        out_specs=[
            pl.BlockSpec((None, hq, tm), lambda b, i: (b, 0, i)),
            pl.BlockSpec((None, tm, hq), lambda b, i: (b, i, 0)),
            pl.BlockSpec((None, None, hv, tm), lambda b, i: (b, i, 0, 0)),
        ],
        out_shape=[
            jax.ShapeDtypeStruct((batch, hq, seq), BF16),
            jax.ShapeDtypeStruct((batch, seq, hq), BF16),
            jax.ShapeDtypeStruct((batch, nblk, hv, tm), BF16),
        ],
        compiler_params=_cparams(("parallel", "parallel")),
        name="mla_prep",
    )(x2d, g, wa, qg, kvg, wqt, wk, place, wvt, ones_col, cos2, sin2, cos_t, sin_t)


def _attn_kernel(qt_ref, qn_ref, k_ref, vt_ref, o_ref, m_ref, acc_ref, s0_ref, c0_ref, *, tk, nk):
    def first_scores(q_ref):
        st = jnp.dot(k_ref[0:tk, :], q_ref[...], preferred_element_type=F32)
        s0_ref[...] = st
        c0_ref[...] = jnp.max(st, axis=0, keepdims=True)

    @pl.when(pl.program_id(2) == 0)
    def _():
        first_scores(qt_ref)

    qt = qt_ref[...]
    m0 = c0_ref[...]
    acc = jnp.dot(vt_ref[0], jnp.exp2(s0_ref[...] - m0).astype(BF16), preferred_element_type=F32)
    for c in range(1, nk):
        st = jnp.dot(k_ref[c * tk:(c + 1) * tk, :], qt, preferred_element_type=F32)
        acc = acc + jnp.dot(vt_ref[c], jnp.exp2(st - m0).astype(BF16), preferred_element_type=F32)
    acc_ref[...] = acc
    first_scores(qn_ref)

    used = acc[:V_DIM + 8]
    unsafe = jnp.max(jnp.where(jnp.abs(used) < ATTN_SAFE_MAX, 0.0, 1.0)) > 0.0

    @pl.when(unsafe)
    def _():
        m_ref[...] = jnp.full(m_ref.shape, -jnp.inf, F32)
        acc_ref[...] = jnp.zeros(acc_ref.shape, F32)

        def body(c, carry):
            k0 = pl.multiple_of(c * tk, tk)
            st = jnp.dot(k_ref[pl.ds(k0, tk), :], qt, preferred_element_type=F32)
            m_old = m_ref[...]
            m_new = jnp.maximum(m_old, jnp.max(st, axis=0, keepdims=True))
            p = jnp.exp2(st - m_new).astype(BF16)
            pv = jnp.dot(vt_ref[c], p, preferred_element_type=F32)
            acc_ref[...] = acc_ref[...] * jnp.exp2(m_old - m_new) + pv
            m_ref[...] = m_new
            return carry

        lax.fori_loop(0, nk, body, 0)

    acc = acc_ref[...]
    o_ref[...] = (acc[:V_DIM] * (1.0 / acc[V_DIM:V_DIM + 1])).astype(o_ref.dtype)


def _attention(qt, k, vt, batch, seq, tq, tk):
    nk = seq // tk
    nq = seq // tq
    return pl.pallas_call(
        functools.partial(_attn_kernel, tk=tk, nk=nk),
        grid=(batch, MLA_HEADS, nq),
        in_specs=[
            pl.BlockSpec((None, QK_PAD, tq), lambda b, h, i: (b, h, i)),
            pl.BlockSpec((None, QK_PAD, tq), lambda b, h, i: (b, h, jnp.minimum(i + 1, nq - 1))),
            pl.BlockSpec((None, seq, QK_PAD), lambda b, h, i: (b, 0, h)),
            pl.BlockSpec((None, nk, VT_ROWS, tk), lambda b, h, i: (b, 0, h, 0)),
        ],
        out_specs=pl.BlockSpec((None, V_DIM, tq), lambda b, h, i: (b, h, i)),
        out_shape=jax.ShapeDtypeStruct((batch, MLA_W, seq), BF16),
        scratch_shapes=[pltpu.VMEM((1, tq), F32), pltpu.VMEM((VT_ROWS, tq), F32),
                        pltpu.VMEM((tk, tq), F32), pltpu.VMEM((1, tq), F32)],
        compiler_params=_cparams(("parallel", "parallel", "arbitrary")),
        name="mla_attention",
    )(qt, qt, k, vt)


def _fill_ext(ext_ref, prev, cur, nxt, i, n_i, ts):
    ext_ref[0:HALO, :] = jnp.where(i > 0, prev, 0.0)
    ext_ref[HALO:HALO + ts, :] = cur
    ext_ref[HALO + ts:2 * HALO + ts, :] = jnp.where(i < n_i - 1, nxt, 0.0)


def _dwconv(ext_ref, w_ref, k_taps, ts, width, emit, rows=128):
    sub = 8
    off = HALO - k_taps // 2
    lo = (off // sub) * sub
    win_rows = ((off - lo + k_taps - 1) // sub + 1) * sub + rows

    def body(rc, carry):
        r0 = pl.multiple_of(rc * rows, rows)
        for cb in range(width // LANES):
            sl = slice(cb * LANES, (cb + 1) * LANES)
            win = ext_ref[pl.ds(r0 + lo, win_rows), sl]
            acc = jnp.zeros((rows, LANES), F32)
            for phase in range(sub):
                taps = [k for k in range(k_taps) if (off - lo + k) % sub == phase]
                if not taps:
                    continue
                shifted = win if phase == 0 else pltpu.roll(win, win_rows - phase, 0)
                for k in taps:
                    q = (off - lo + k) // sub * sub
                    acc = acc + w_ref[k:k + 1, sl] * shifted[q:q + rows]
            emit(r0, sl, acc)
        return carry

    lax.fori_loop(0, ts // rows, body, 0)


def _halo_specs(ts, width, col, nblk, batch):
    per = ts // HALO
    last = batch * nblk * per - 1

    def prev_map(b, i, *_):
        return (jnp.maximum((b * nblk + i) * per - 1, 0), col(*_))

    def cur_map(b, i, *_):
        return (b * nblk + i, col(*_))

    def next_map(b, i, *_):
        return (jnp.minimum((b * nblk + i + 1) * per, last), col(*_))

    return [pl.BlockSpec((HALO, width), prev_map), pl.BlockSpec((ts, width), cur_map),
            pl.BlockSpec((HALO, width), next_map)]


def _conformer_kernel(ap_ref, a_ref, an_ref, gp_ref, g_ref, gn_ref, z_ref, w_ref, b_ref, lng_ref,
                      lnb_ref, o_ref, ext_ref, conv_ref, *, ts):
    i = pl.program_id(1)
    n_i = pl.num_programs(1)

    def glu(a, g):
        return a.astype(F32) * (1.0 / (1.0 + jnp.exp(-g.astype(F32))))

    _fill_ext(ext_ref, glu(ap_ref[...], gp_ref[...]), glu(a_ref[...], g_ref[...]),
              glu(an_ref[...], gn_ref[...]), i, n_i, ts)

    def emit(r0, sl, acc):
        conv_ref[pl.ds(r0, acc.shape[0]), sl] = acc + b_ref[:, sl]

    _dwconv(ext_ref, w_ref, CONV_K, ts, CONV_W, emit)

    rows = 128

    def ln_body(rc, carry):
        r0 = pl.multiple_of(rc * rows, rows)
        v = conv_ref[pl.ds(r0, rows), :]
        vc = v - jnp.mean(v, axis=-1, keepdims=True)
        var = jnp.mean(vc * vc, axis=-1, keepdims=True)
        y = vc * lax.rsqrt(var + EPS) * lng_ref[...] + lnb_ref[...]
        y = _silu(y) * _silu(z_ref[pl.ds(r0, rows), :].astype(F32))
        o_ref[pl.ds(r0, rows), :] = y.astype(o_ref.dtype)
        return carry

    lax.fori_loop(0, ts // rows, ln_body, 0)


def _conformer(u2, w, b, lng, lnb, batch, seq, ts):
    n = batch * seq
    nblk = seq // ts
    a_specs = _halo_specs(ts, CONV_W, lambda: 1, nblk, batch)
    g_specs = _halo_specs(ts, CONV_W, lambda: 2, nblk, batch)
    small = lambda arr: pl.BlockSpec(arr.shape, lambda b_, i: (0, 0))
    return pl.pallas_call(
        functools.partial(_conformer_kernel, ts=ts),
        grid=(batch, nblk),
        in_specs=a_specs + g_specs + [
            pl.BlockSpec((ts, CONV_W), lambda b_, i: (b_ * nblk + i, 3)),
            small(w), small(b), small(lng), small(lnb)],
        out_specs=pl.BlockSpec((ts, CONV_W), lambda b_, i: (b_ * nblk + i, 0)),
        out_shape=jax.ShapeDtypeStruct((n, CONV_W), BF16),
        scratch_shapes=[pltpu.VMEM((ts + 2 * HALO, CONV_W), F32), pltpu.VMEM((ts, CONV_W), F32)],
        compiler_params=_cparams(("parallel", "parallel")),
        name="conformer_conv",
    )(u2, u2, u2, u2, u2, u2, u2, w, b, lng, lnb)


def _out_even_kernel(x_ref, ot_ref, z_ref, oa_ref, w1_ref, w2_ref, o_ref):
    ob = ot_ref[...].astype(F32).T * _silu(z_ref[...].astype(F32))
    acc = jnp.dot(ob.astype(BF16), w1_ref[...], preferred_element_type=F32)
    acc = acc + jnp.dot(oa_ref[...], w2_ref[...], preferred_element_type=F32)
    o_ref[...] = x_ref[...] + acc


def _out_even(x2d, ot, u2, oa, w1, w2, batch, seq, tm):
    n = batch * seq
    nblk = seq // tm
    return pl.pallas_call(
        _out_even_kernel,
        grid=(batch, nblk),
        in_specs=[
            pl.BlockSpec((tm, D_MODEL), lambda b, i: (b * nblk + i, 0)),
            pl.BlockSpec((None, MLA_W, tm), lambda b, i: (b, 0, i)),
            pl.BlockSpec((tm, MLA_W), lambda b, i: (b * nblk + i, 0)),
            pl.BlockSpec((tm, CONV_W), lambda b, i: (b * nblk + i, 0)),
            pl.BlockSpec(w1.shape, lambda b, i: (0, 0)),
            pl.BlockSpec(w2.shape, lambda b, i: (0, 0)),
        ],
        out_specs=pl.BlockSpec((tm, D_MODEL), lambda b, i: (b * nblk + i, 0)),
        out_shape=jax.ShapeDtypeStruct((n, D_MODEL), F32),
        compiler_params=_cparams(("parallel", "parallel")),
        name="out_proj_even",
    )(x2d, ot, u2, oa, w1, w2)


def _dt_kernel(x_ref, g_ref, w_ref, o_ref):
    xn = _rms(x_ref[...], g_ref[...]).astype(BF16)
    o_ref[...] = lax.dot_general(w_ref[...], xn, (((1,), (1,)), ((), ())), preferred_element_type=F32)


def _dt_proj(x2d, g, wdt_t, tm):
    n = x2d.shape[0]
    rows = wdt_t.shape[0]
    return pl.pallas_call(
        _dt_kernel,
        grid=(n // tm,),
        in_specs=[pl.BlockSpec((tm, D_MODEL), lambda i: (i, 0)),
                  pl.BlockSpec((1, D_MODEL), lambda i: (0, 0)),
                  pl.BlockSpec(wdt_t.shape, lambda i: (0, 0))],
        out_specs=pl.BlockSpec((rows, tm), lambda i: (0, i)),
        out_shape=jax.ShapeDtypeStruct((rows, n), F32),
        compiler_params=_cparams(("parallel",)),
        name="dt_proj",
    )(x2d, g, wdt_t)


def _xbc_conv_kernel(p_ref, c_ref, n_ref, w_ref, b_ref, o_ref, ext_ref, *, ts, width):
    i = pl.program_id(1)
    _fill_ext(ext_ref, p_ref[...].astype(F32), c_ref[...].astype(F32), n_ref[...].astype(F32),
              i, pl.num_programs(1), ts)

    def emit(r0, sl, acc):
        o_ref[pl.ds(r0, acc.shape[0]), sl] = _silu(acc + b_ref[:, sl]).astype(o_ref.dtype)

    _dwconv(ext_ref, w_ref, SSD_CONV_K, ts, width, emit)


def _xbc_conv(uo, w, b, batch, seq, ts, width, col0):
    n = batch * seq
    nblk = seq // ts
    ncol = XBC_W // width
    specs = _halo_specs(ts, width, lambda j: col0 + j, nblk, batch)
    return pl.pallas_call(
        functools.partial(_xbc_conv_kernel, ts=ts, width=width),
        grid=(batch, nblk, ncol),
        in_specs=specs + [pl.BlockSpec((SSD_CONV_K, width), lambda b_, i, j: (0, j)),
                          pl.BlockSpec((1, width), lambda b_, i, j: (0, j))],
        out_specs=pl.BlockSpec((ts, width), lambda b_, i, j: (b_ * nblk + i, j)),
        out_shape=jax.ShapeDtypeStruct((n, XBC_W), BF16),
        scratch_shapes=[pltpu.VMEM((ts + 2 * HALO, width), F32)],
        compiler_params=_cparams(("parallel", "parallel", "parallel")),
        name="xbc_conv",
    )(uo, uo, uo, w, b)


def _ssd_kernel(xbc_ref, dt_ref, bias_ref, alog_ref, y_ref, state_ref):
    d = pl.program_id(1)
    c = pl.program_id(2)
    L = SSD_CHUNK
    G = SSD_GROUPS
    pairs = SSD_HPG // 2
    gw = SSD_HPG * SSD_HEAD_DIM

    @pl.when(c == 0)
    def _():
        state_ref[...] = jnp.zeros(state_ref.shape, F32)

    row = lax.broadcasted_iota(jnp.int32, (L, L), 0)
    col = lax.broadcasted_iota(jnp.int32, (L, L), 1)
    lane_lo = col < SSD_HEAD_DIM
    ahead = (row - col) * (1 - 2 * d)
    mask = ahead >= 0

    stack_rows = lambda ref: jnp.concatenate([ref[g] for g in range(G)], axis=0)
    dtv = _softplus(stack_rows(dt_ref) + stack_rows(bias_ref))
    da = dtv * (-math.log2(math.e) * jnp.exp(stack_rows(alog_ref)))
    tri = jnp.where(ahead <= 0, 1.0, 0.0)
    cs_t = jnp.dot(da, tri, preferred_element_type=F32, precision=lax.Precision.HIGHEST)
    tot = jnp.sum(da, axis=1, keepdims=True)
    etot = jnp.exp2(jnp.broadcast_to(tot, cs_t.shape))
    w_t = jnp.exp2(tot - cs_t) * dtv
    srow_t = cs_t - jnp.log2(dtv)
    cs_cols = jnp.concatenate([cs_t, jnp.zeros((L - 8 * G, L), F32)], axis=0).T

    for g in range(G):
        bm = xbc_ref[:, SSD_W + g * SSD_STATE:SSD_W + (g + 1) * SSD_STATE]
        cm = xbc_ref[:, SSD_W + (G + g) * SSD_STATE:SSD_W + (G + g + 1) * SSD_STATE]
        cb = lax.dot_general(cm, bm, (((1,), (1,)), ((), ())), preferred_element_type=F32)
        bt = bm.astype(F32).T
        cmf = cm.astype(F32)
        for j in range(pairs):
            lanes = slice(g * gw + j * LANES, g * gw + (j + 1) * LANES)
            xs_pair = xbc_ref[:, lanes]
            st_pair = state_ref[g * pairs + j]
            rhs = jnp.concatenate([xs_pair, st_pair.astype(BF16)], axis=0)
            ys = []
            news = []
            for r in (8 * g + 2 * j, 8 * g + 2 * j + 1):
                cs_col = jnp.broadcast_to(cs_cols[:, r:r + 1], (L, L))
                m = jnp.where(mask, jnp.exp2(cs_col - srow_t[r:r + 1, :]), 0.0) * cb
                cs_scaled = cmf * jnp.exp2(cs_col)
                lhs = jnp.concatenate([m.astype(BF16), cs_scaled.astype(BF16)], axis=1)
                ys.append(jnp.dot(lhs, rhs, preferred_element_type=F32))
                btw = (bt * w_t[r:r + 1, :]).astype(BF16)
                news.append(jnp.dot(btw, xs_pair, preferred_element_type=F32))
            y_ref[:, lanes] = jnp.where(lane_lo, ys[0], ys[1]).astype(y_ref.dtype)
            r0 = 8 * g + 2 * j
            decay = jnp.where(lane_lo[:1], etot[r0:r0 + 1, :], etot[r0 + 1:r0 + 2, :])
            state_ref[g * pairs + j] = st_pair * decay + jnp.where(lane_lo, news[0], news[1])


def _ssd(xbc, dt_t, bias, alog, batch, seq):
    n = batch * seq
    nc = seq // SSD_CHUNK

    def tpos(b, d, c):
        return b * nc + jnp.where(d == 0, c, nc - 1 - c)

    head_rows = (None, SSD_GROUPS, 8, SSD_CHUNK)
    return pl.pallas_call(
        _ssd_kernel,
        grid=(batch, 2, nc),
        in_specs=[
            pl.BlockSpec((SSD_CHUNK, XBC_W), lambda b, d, c: (tpos(b, d, c), 0)),
            pl.BlockSpec(head_rows, lambda b, d, c: (d, 0, 0, tpos(b, d, c))),
            pl.BlockSpec(head_rows, lambda b, d, c: (d, 0, 0, 0)),
            pl.BlockSpec(head_rows, lambda b, d, c: (d, 0, 0, 0)),
        ],
        out_specs=pl.BlockSpec((None, SSD_CHUNK, SSD_W), lambda b, d, c: (d, tpos(b, d, c), 0)),
        out_shape=jax.ShapeDtypeStruct((2, n, SSD_W), BF16),
        scratch_shapes=[pltpu.VMEM((SSD_GROUPS * SSD_HPG // 2, SSD_STATE, LANES), F32)],
        compiler_params=_cparams(("parallel", "arbitrary", "arbitrary")),
        name="ssd_scan",
    )(xbc, dt_t, bias, alog)


def _ssd_gate_kernel(y_ref, xs_ref, z_ref, dskip_ref, g_ref, o_ref):
    gw = SSD_HPG * SSD_HEAD_DIM
    y = y_ref[0].astype(F32) + y_ref[1].astype(F32) + dskip_ref[...] * xs_ref[...].astype(F32)
    y = y * _silu(z_ref[...].astype(F32))
    for g in range(SSD_GROUPS):
        sl = slice(g * gw, (g + 1) * gw)
        o_ref[:, sl] = _rms(y[:, sl], g_ref[:, sl]).astype(o_ref.dtype)


def _ssd_gate(y2, xbc, uo, dskip, g, tm):
    n = xbc.shape[0]
    return pl.pallas_call(
        _ssd_gate_kernel,
        grid=(n // tm,),
        in_specs=[
            pl.BlockSpec((2, tm, SSD_W), lambda i: (0, i, 0)),
            pl.BlockSpec((tm, SSD_W), lambda i: (i, 0)),
            pl.BlockSpec((tm, SSD_W), lambda i: (i, 0)),
            pl.BlockSpec((1, SSD_W), lambda i: (0, 0)),
            pl.BlockSpec((1, SSD_W), lambda i: (0, 0)),
        ],
        out_specs=pl.BlockSpec((tm, SSD_W), lambda i: (i, 0)),
        out_shape=jax.ShapeDtypeStruct((n, SSD_W), BF16),
        compiler_params=_cparams(("parallel",)),
        name="ssd_gate_norm",
    )(y2, xbc, uo, dskip, g)


def _short_conv_kernel(gcp_ref, gc_ref, gcn_ref, hp_ref, h_ref, hn_ref, gb_ref, z_ref, w_ref,
                       o_ref, ext_ref, *, ts):
    i = pl.program_id(1)
    mul = lambda a, b: a.astype(F32) * b.astype(F32)
    _fill_ext(ext_ref, mul(gcp_ref[...], hp_ref[...]), mul(gc_ref[...], h_ref[...]),
              mul(gcn_ref[...], hn_ref[...]), i, pl.num_programs(1), ts)

    def emit(r0, sl, acc):
        rs = pl.ds(r0, acc.shape[0])
        y = gb_ref[rs, sl].astype(F32) * acc * _silu(z_ref[rs, sl].astype(F32))
        o_ref[rs, sl] = y.astype(o_ref.dtype)

    _dwconv(ext_ref, w_ref, SC_K, ts, SC_W, emit)


def _short_conv(uo, w, batch, seq, ts, col0):
    n = batch * seq
    nblk = seq // ts
    gc_specs = _halo_specs(ts, SC_W, lambda: col0 + 1, nblk, batch)
    h_specs = _halo_specs(ts, SC_W, lambda: col0 + 2, nblk, batch)
    return pl.pallas_call(
        functools.partial(_short_conv_kernel, ts=ts),
        grid=(batch, nblk),
        in_specs=gc_specs + h_specs + [
            pl.BlockSpec((ts, SC_W), lambda b, i: (b * nblk + i, col0)),
            pl.BlockSpec((ts, SC_W), lambda b, i: (b * nblk + i, col0 + 3)),
            pl.BlockSpec(w.shape, lambda b, i: (0, 0))],
        out_specs=pl.BlockSpec((ts, SC_W), lambda b, i: (b * nblk + i, 0)),
        out_shape=jax.ShapeDtypeStruct((n, SC_W), BF16),
        scratch_shapes=[pltpu.VMEM((ts + 2 * HALO, SC_W), F32)],
        compiler_params=_cparams(("parallel", "parallel")),
        name="short_conv",
    )(uo, uo, uo, uo, uo, uo, uo, uo, w)


def _out_odd_kernel(x_ref, oc_ref, od_ref, w1_ref, w2_ref, *rest):
    o_ref = rest[-1]
    acc = jnp.dot(oc_ref[...], w1_ref[...], preferred_element_type=F32)
    acc = acc + jnp.dot(od_ref[...], w2_ref[...], preferred_element_type=F32)
    y = x_ref[...] + acc
    if len(rest) == 2:
        y = _rms(y, rest[0][...])
    o_ref[...] = y


def _out_odd(x2d, oc, od, w1, w2, final_g, tm):
    n = x2d.shape[0]
    extra = [] if final_g is None else [final_g.reshape(1, -1).astype(F32)]
    return pl.pallas_call(
        _out_odd_kernel,
        grid=(n // tm,),
        in_specs=[
            pl.BlockSpec((tm, D_MODEL), lambda i: (i, 0)),
            pl.BlockSpec((tm, SSD_W), lambda i: (i, 0)),
            pl.BlockSpec((tm, SC_W), lambda i: (i, 0)),
            pl.BlockSpec(w1.shape, lambda i: (0, 0)),
            pl.BlockSpec(w2.shape, lambda i: (0, 0)),
        ] + [pl.BlockSpec((1, D_MODEL), lambda i: (0, 0)) for _ in extra],
        out_specs=pl.BlockSpec((tm, D_MODEL), lambda i: (i, 0)),
        out_shape=jax.ShapeDtypeStruct((n, D_MODEL), F32),
        compiler_params=_cparams(("parallel",)),
        name="out_proj_odd",
    )(x2d, oc, od, w1, w2, *extra)


def _final_norm_kernel(x_ref, g_ref, o_ref):
    o_ref[...] = _rms(x_ref[...], g_ref[...])


def _final_norm(x2d, g, tm):
    n = x2d.shape[0]
    return pl.pallas_call(
        _final_norm_kernel,
        grid=(n // tm,),
        in_specs=[pl.BlockSpec((tm, D_MODEL), lambda i: (i, 0)), pl.BlockSpec((1, D_MODEL), lambda i: (0, 0))],
        out_specs=pl.BlockSpec((tm, D_MODEL), lambda i: (i, 0)),
        out_shape=jax.ShapeDtypeStruct((n, D_MODEL), F32),
        compiler_params=_cparams(("parallel",)),
        name="final_norm",
    )(x2d, g)


def _tile(total, want):
    return want if total % want == 0 else total


def _even_layer(x2d, batch, seq, norm_g, w_in, w_uq, q_g, w_ukv, kv_g, conv_w, conv_b, ln_g, ln_b, w_out,
                tables):
    n = batch * seq
    low = Q_LORA + KV_LORA + QK_ROPE
    row = lambda v: v.reshape(1, -1).astype(F32)

    wa = jnp.pad(w_in[:, :low], ((0, 0), (0, 6 * LANES - low))).astype(BF16)
    w_rest = w_in[:, low:].astype(BF16)
    wq = w_uq.reshape(Q_LORA, MLA_HEADS, QK_NOPE + QK_ROPE)
    wqt = jnp.pad(wq, ((0, 0), (0, 0), (0, QK_PAD - QK_NOPE - QK_ROPE)))
    wqt = wqt.reshape(Q_LORA, MLA_HEADS * QK_PAD).T.astype(BF16)
    wkv = w_ukv.reshape(KV_LORA, MLA_HEADS, QK_NOPE + V_DIM)
    wk = jnp.pad(wkv[:, :, :QK_NOPE], ((0, 0), (0, 0), (0, QK_PAD - QK_NOPE)))
    wk = wk.reshape(KV_LORA, MLA_HEADS * QK_PAD).astype(BF16)
    wvt = jnp.pad(wkv[:, :, QK_NOPE:], ((0, 0), (0, 0), (0, VT_ROWS - V_DIM)))
    wvt = wvt.reshape(KV_LORA, MLA_HEADS * VT_ROWS).T.astype(BF16)
    cols = jnp.arange(MLA_HEADS * QK_PAD)
    place = ((cols[None, :] % QK_PAD) == (jnp.arange(LANES)[:, None] + QK_NOPE))
    place = (place & (jnp.arange(LANES)[:, None] < QK_ROPE)).astype(BF16)
    ones_col = ((jnp.arange(MLA_HEADS * VT_ROWS) % VT_ROWS) == V_DIM).astype(F32)[:, None]

    tm = _tile(seq, TOKEN_TILE)
    u2 = _rms_matmul(x2d, row(norm_g), w_rest, _tile(n, PROJ_ROWS), _tile(w_rest.shape[1], PROJ_COLS),
                     BF16)
    qt, k, vt = _mla_prep(x2d, row(norm_g), wa, row(q_g), row(kv_g), wqt, wk, place, wvt, ones_col,
                          *tables, batch, seq, tm)
    ot = _attention(qt, k, vt, batch, seq, _tile(seq, QUERY_TILE), tm)
    oa = _conformer(u2, conv_w.astype(F32), row(conv_b), row(ln_g), row(ln_b), batch, seq, tm)
    w_out = w_out.astype(BF16)
    return _out_even(x2d, ot, u2, oa, w_out[:MLA_W], w_out[MLA_W:], batch, seq, _tile(seq, OUT_ROWS))


def _odd_layer(x2d, batch, seq, norm_g, w_in, conv_c_w, conv_c_b, dt_bias_f, dt_bias_b, a_log_f, a_log_b,
               d_skip, ssd_g, conv_d_w, w_out, final_g):
    n = batch * seq
    row = lambda v: v.reshape(1, -1).astype(F32)
    o_dt = SSD_W + XBC_W
    w_main = jnp.concatenate([w_in[:, :o_dt], w_in[:, o_dt + 2 * SSD_HEADS:]], axis=1).astype(BF16)
    w_dt = w_in[:, o_dt:o_dt + 2 * SSD_HEADS].reshape(D_MODEL, 2 * SSD_GROUPS, SSD_HPG)
    wdt_t = jnp.pad(w_dt, ((0, 0), (0, 0), (0, 8 - SSD_HPG))).reshape(D_MODEL, 64).T.astype(BF16)

    def per_head(vf, vb):
        v = jnp.concatenate([vf, vb]).reshape(2, SSD_GROUPS, SSD_HPG)
        v = jnp.pad(v, ((0, 0), (0, 0), (0, 8 - SSD_HPG)))
        return jnp.broadcast_to(v[..., None], (2, SSD_GROUPS, 8, SSD_CHUNK)).astype(F32)

    tm = _tile(seq, TOKEN_TILE)
    uo = _rms_matmul(x2d, row(norm_g), w_main, _tile(n, PROJ_ROWS), _tile(w_main.shape[1], PROJ_COLS),
                     BF16)
    dt_t = _dt_proj(x2d, row(norm_g), wdt_t, _tile(n, PROJ_ROWS)).reshape(2, SSD_GROUPS, 8, n)
    xbc = _xbc_conv(uo, conv_c_w.astype(F32), row(conv_c_b), batch, seq, tm, 512, SSD_W // 512)
    y2 = _ssd(xbc, dt_t, per_head(dt_bias_f, dt_bias_b), per_head(a_log_f, a_log_b), batch, seq)
    dskip = row(jnp.repeat(d_skip, SSD_HEAD_DIM))
    oc = _ssd_gate(y2, xbc, uo, dskip, row(ssd_g), tm)
    od = _short_conv(uo, conv_d_w.astype(F32), batch, seq, tm, (SSD_W + XBC_W) // SC_W)
    w_out = w_out.astype(BF16)
    return _out_odd(x2d, oc, od, w_out[:SSD_W], w_out[SSD_W:], final_g, _tile(n, OUT_ROWS))


def _rope_tables(seq):
    half = QK_ROPE // 2
    inv = ROPE_THETA ** (-jnp.arange(half, dtype=F32) / half)
    ang = jnp.arange(seq, dtype=F32)[:, None] * inv[None, :]
    cos, sin = jnp.cos(ang), jnp.sin(ang)
    pad = ((0, 0), (0, LANES - QK_ROPE))
    cos2 = jnp.pad(jnp.concatenate([cos, cos], axis=1), pad)
    sin2 = jnp.pad(jnp.concatenate([-sin, sin], axis=1), pad)
    return cos2, sin2, cos.T, sin.T


def kernel(x, norm_e, w_in_e, w_uq, q_norm, w_ukv, kv_norm, conv_a_w, conv_a_b, ln_a_g, ln_a_b, w_out_e, norm_o, w_in_o, conv_c_w, conv_c_b, dt_bias_f, dt_bias_b, a_log_f, a_log_b, d_skip, ssd_norm, conv_d_w, w_out_o, final_norm):
    batch, seq, _ = x.shape
    depth = norm_e.shape[0] + norm_o.shape[0]
    tables = _rope_tables(seq)
    x2d = x.reshape(batch * seq, D_MODEL)
    for i in range(depth):
        j = i // 2
        if i % 2 == 0:
            x2d = _even_layer(x2d, batch, seq, norm_e[j], w_in_e[j], w_uq[j], q_norm[j], w_ukv[j], kv_norm[j],
                              conv_a_w[j], conv_a_b[j], ln_a_g[j], ln_a_b[j], w_out_e[j], tables)
        else:
            last = i == depth - 1
            x2d = _odd_layer(x2d, batch, seq, norm_o[j], w_in_o[j], conv_c_w[j], conv_c_b[j], dt_bias_f[j],
                             dt_bias_b[j], a_log_f[j], a_log_b[j], d_skip[j], ssd_norm[j], conv_d_w[j],
                             w_out_o[j], final_norm if last else None)
    if depth % 2 == 1:
        x2d = _final_norm(x2d, final_norm.reshape(1, -1), _tile(batch * seq, TOKEN_TILE))
    return x2d.reshape(batch, seq, D_MODEL)
```

```python
import functools
import math

import jax
import jax.numpy as jnp
from jax import lax
from jax.experimental import pallas as pl
from jax.experimental.pallas import tpu as pltpu

F32 = jnp.float32
BF16 = jnp.bfloat16
EPS = 1e-6

D_MODEL = 1024
MLA_HEADS = 16
QK_NOPE = 64
QK_ROPE = 32
V_DIM = 64
Q_LORA = 384
KV_LORA = 256
MLA_W = MLA_HEADS * V_DIM
ROPE_THETA = 10000.0
CONV_W = 1024
CONV_K = 31
SSD_HEAD_DIM = 64
SSD_HEADS = 24
SSD_W = SSD_HEADS * SSD_HEAD_DIM
SSD_GROUPS = 4
SSD_HPG = SSD_HEADS // SSD_GROUPS
SSD_STATE = 128
SSD_CONV_K = 5
SSD_CHUNK = 128
XBC_W = SSD_W + 2 * SSD_GROUPS * SSD_STATE
SC_W = 512
SC_K = 3

LANES = 128
HALO = 16
QK_PAD = 128
VT_ROWS = 80
ATTN_SAFE_MAX = 1e30
VMEM_LIMIT = 56 * 1024 * 1024

TOKEN_TILE = 512
CONV_TILE = 1024
QUERY_TILE = 2048
PROJ_ROWS = 1024
PROJ_COLS = 2048
OUT_ROWS = 1024
SSD_CHUNKS_PER_STEP = 4


def _cparams(sem):
    return pltpu.CompilerParams(dimension_semantics=sem, vmem_limit_bytes=VMEM_LIMIT)


def _silu(v):
    return v * (1.0 / (1.0 + jnp.exp(-v)))


def _softplus(v):
    return jnp.maximum(v, 0.0) + jnp.log(1.0 + jnp.exp(-jnp.abs(v)))


def _rms(v, g):
    ms = jnp.mean(v * v, axis=-1, keepdims=True)
    return v * lax.rsqrt(ms + EPS) * g


def _rms_matmul_kernel(x_ref, g_ref, w_ref, o_ref, xn_ref):
    @pl.when(pl.program_id(1) == 0)
    def _():
        xn_ref[...] = _rms(x_ref[...], g_ref[...]).astype(xn_ref.dtype)

    o_ref[...] = jnp.dot(xn_ref[...], w_ref[...], preferred_element_type=F32).astype(o_ref.dtype)


def _rms_matmul(x, g, w, tm, tn, out_dtype):
    n, d = x.shape
    e = w.shape[1]
    return pl.pallas_call(
        _rms_matmul_kernel,
        grid=(n // tm, e // tn),
        in_specs=[
            pl.BlockSpec((tm, d), lambda i, j: (i, 0)),
            pl.BlockSpec((1, d), lambda i, j: (0, 0)),
            pl.BlockSpec((d, tn), lambda i, j: (0, j)),
        ],
        out_specs=pl.BlockSpec((tm, tn), lambda i, j: (i, j)),
        out_shape=jax.ShapeDtypeStruct((n, e), out_dtype),
        scratch_shapes=[pltpu.VMEM((tm, d), BF16)],
        compiler_params=_cparams(("parallel", "arbitrary")),
        name="rms_matmul",
    )(x, g, w)


def _mla_prep_kernel(x_ref, g_ref, wa_ref, qg_ref, kvg_ref, wqt_ref, wk_ref, place_ref, wvt_ref,
                     ones_ref, cos2_ref, sin2_ref, cost_ref, sint_ref,
                     qt_ref, k_ref, vt_ref, *, qscale):
    xn = _rms(x_ref[...], g_ref[...]).astype(BF16)
    a = jnp.dot(xn, wa_ref[...], preferred_element_type=F32)
    qn = _rms(a[:, :Q_LORA], qg_ref[...]).astype(BF16)
    kvn = _rms(a[:, Q_LORA:Q_LORA + KV_LORA], kvg_ref[...]).astype(BF16)

    kr = a[:, Q_LORA + KV_LORA:]
    lane = lax.broadcasted_iota(jnp.int32, kr.shape, 1)
    half = QK_ROPE // 2
    swapped = jnp.where(lane < half, pltpu.roll(kr, LANES - half, 1), pltpu.roll(kr, half, 1))
    kr = kr * cos2_ref[...] + swapped * sin2_ref[...]

    k = jnp.dot(kvn, wk_ref[...], preferred_element_type=F32)
    k = k + jnp.dot(kr.astype(BF16), place_ref[...], preferred_element_type=F32)
    k_ref[...] = k.astype(k_ref.dtype)

    nt = (((1,), (1,)), ((), ()))
    vt = lax.dot_general(wvt_ref[...], kvn, nt, preferred_element_type=F32) + ones_ref[...]
    vt_ref[...] = vt.astype(vt_ref.dtype)

    qt = lax.dot_general(wqt_ref[...], qn, nt, preferred_element_type=F32) * qscale
    qt_ref[...] = qt.astype(qt_ref.dtype)
    cos_t = cost_ref[...]
    sin_t = sint_ref[...]
    for h in range(MLA_HEADS):
        r0 = h * QK_PAD + QK_NOPE
        x1 = qt[r0:r0 + half]
        x2 = qt[r0 + half:r0 + 2 * half]
        qt_ref[r0:r0 + half, :] = (x1 * cos_t - x2 * sin_t).astype(qt_ref.dtype)
        qt_ref[r0 + half:r0 + 2 * half, :] = (x2 * cos_t + x1 * sin_t).astype(qt_ref.dtype)


def _mla_prep(x2d, g, wa, qg, kvg, wqt, wk, place, wvt, ones_col, cos2, sin2, cos_t, sin_t,
              batch, seq, tm):
    nblk = seq // tm
    qscale = (QK_NOPE + QK_ROPE) ** -0.5 * math.log2(math.e)
    full = lambda arr: pl.BlockSpec(arr.shape, lambda b, i: (0,) * arr.ndim)
    hq = MLA_HEADS * QK_PAD
    hv = MLA_HEADS * VT_ROWS
    return pl.pallas_call(
        functools.partial(_mla_prep_kernel, qscale=qscale),
        grid=(batch, nblk),
        in_specs=[
            pl.BlockSpec((tm, D_MODEL), lambda b, i: (b * nblk + i, 0)),
            full(g), full(wa), full(qg), full(kvg), full(wqt), full(wk), full(place), full(wvt),
            full(ones_col),
            pl.BlockSpec((tm, LANES), lambda b, i: (i, 0)),
            pl.BlockSpec((tm, LANES), lambda b, i: (i, 0)),
            pl.BlockSpec((QK_ROPE // 2, tm), lambda b, i: (0, i)),
            pl.BlockSpec((QK_ROPE // 2, tm), lambda b, i: (0, i)),
        ],
        out_specs=[
            pl.BlockSpec((None, hq, tm), lambda b, i: (b, 0, i)),
            pl.BlockSpec((None, tm, hq), lambda b, i: (b, i, 0)),
            pl.BlockSpec((None, None, hv, tm), lambda b, i: (b, i, 0, 0)),
        ],
        out_shape=[
            jax.ShapeDtypeStruct((batch, hq, seq), BF16),
            jax.ShapeDtypeStruct((batch, seq, hq), BF16),
            jax.ShapeDtypeStruct((batch, nblk, hv, tm), BF16),
        ],
        compiler_params=_cparams(("parallel", "parallel")),
        name="mla_prep",
    )(x2d, g, wa, qg, kvg, wqt, wk, place, wvt, ones_col, cos2, sin2, cos_t, sin_t)


def _attn_kernel(qt_ref, qn_ref, k_ref, vt_ref, o_ref, m_ref, acc_ref, s0_ref, c0_ref, *, tk, nk):
    def first_scores(q_ref):
        st = jnp.dot(k_ref[0:tk, :], q_ref[...], preferred_element_type=F32)
        s0_ref[...] = st
        c0_ref[...] = jnp.max(st, axis=0, keepdims=True)

    @pl.when(pl.program_id(2) == 0)
    def _():
        first_scores(qt_ref)

    qt = qt_ref[...]
    m0 = c0_ref[...]
    acc = jnp.dot(vt_ref[0], jnp.exp2(s0_ref[...] - m0).astype(BF16), preferred_element_type=F32)
    for c in range(1, nk):
        st = jnp.dot(k_ref[c * tk:(c + 1) * tk, :], qt, preferred_element_type=F32)
        acc = acc + jnp.dot(vt_ref[c], jnp.exp2(st - m0).astype(BF16), preferred_element_type=F32)
    acc_ref[...] = acc
    first_scores(qn_ref)

    used = acc[:V_DIM + 8]
    unsafe = jnp.max(jnp.where(jnp.abs(used) < ATTN_SAFE_MAX, 0.0, 1.0)) > 0.0

    @pl.when(unsafe)
    def _():
        m_ref[...] = jnp.full(m_ref.shape, -jnp.inf, F32)
        acc_ref[...] = jnp.zeros(acc_ref.shape, F32)

        def body(c, carry):
            k0 = pl.multiple_of(c * tk, tk)
            st = jnp.dot(k_ref[pl.ds(k0, tk), :], qt, preferred_element_type=F32)
            m_old = m_ref[...]
            m_new = jnp.maximum(m_old, jnp.max(st, axis=0, keepdims=True))
            p = jnp.exp2(st - m_new).astype(BF16)
            pv = jnp.dot(vt_ref[c], p, preferred_element_type=F32)
            acc_ref[...] = acc_ref[...] * jnp.exp2(m_old - m_new) + pv
            m_ref[...] = m_new
            return carry

        lax.fori_loop(0, nk, body, 0)

    acc = acc_ref[...]
    o_ref[...] = (acc[:V_DIM] * (1.0 / acc[V_DIM:V_DIM + 1])).astype(o_ref.dtype)


def _attention(qt, k, vt, batch, seq, tq, tk):
    nk = seq // tk
    nq = seq // tq
    return pl.pallas_call(
        functools.partial(_attn_kernel, tk=tk, nk=nk),
        grid=(batch, MLA_HEADS, nq),
        in_specs=[
            pl.BlockSpec((None, QK_PAD, tq), lambda b, h, i: (b, h, i)),
            pl.BlockSpec((None, QK_PAD, tq), lambda b, h, i: (b, h, jnp.minimum(i + 1, nq - 1))),
            pl.BlockSpec((None, seq, QK_PAD), lambda b, h, i: (b, 0, h)),
            pl.BlockSpec((None, nk, VT_ROWS, tk), lambda b, h, i: (b, 0, h, 0)),
        ],
        out_specs=pl.BlockSpec((None, V_DIM, tq), lambda b, h, i: (b, h, i)),
        out_shape=jax.ShapeDtypeStruct((batch, MLA_W, seq), BF16),
        scratch_shapes=[pltpu.VMEM((1, tq), F32), pltpu.VMEM((VT_ROWS, tq), F32),
                        pltpu.VMEM((tk, tq), F32), pltpu.VMEM((1, tq), F32)],
        compiler_params=_cparams(("parallel", "parallel", "arbitrary")),
        name="mla_attention",
    )(qt, qt, k, vt)


def _fill_ext(ext_ref, prev, cur, nxt, i, n_i, ts):
    ext_ref[0:HALO, :] = jnp.where(i > 0, prev, 0.0)
    ext_ref[HALO:HALO + ts, :] = cur
    ext_ref[HALO + ts:2 * HALO + ts, :] = jnp.where(i < n_i - 1, nxt, 0.0)


def _dwconv(ext_ref, w_ref, k_taps, ts, width, emit, rows=128):
    sub = 8
    off = HALO - k_taps // 2
    lo = (off // sub) * sub
    win_rows = ((off - lo + k_taps - 1) // sub + 1) * sub + rows

    def body(rc, carry):
        r0 = pl.multiple_of(rc * rows, rows)
        for cb in range(width // LANES):
            sl = slice(cb * LANES, (cb + 1) * LANES)
            win = ext_ref[pl.ds(r0 + lo, win_rows), sl]
            acc = jnp.zeros((rows, LANES), F32)
            for phase in range(sub):
                taps = [k for k in range(k_taps) if (off - lo + k) % sub == phase]
                if not taps:
                    continue
                shifted = win if phase == 0 else pltpu.roll(win, win_rows - phase, 0)
                for k in taps:
                    q = (off - lo + k) // sub * sub
                    acc = acc + w_ref[k:k + 1, sl] * shifted[q:q + rows]
            emit(r0, sl, acc)
        return carry

    lax.fori_loop(0, ts // rows, body, 0)


def _halo_specs(ts, width, col, nblk, batch):
    per = ts // HALO
    last = batch * nblk * per - 1

    def prev_map(b, i, *_):
        return (jnp.maximum((b * nblk + i) * per - 1, 0), col(*_))

    def cur_map(b, i, *_):
        return (b * nblk + i, col(*_))

    def next_map(b, i, *_):
        return (jnp.minimum((b * nblk + i + 1) * per, last), col(*_))

    return [pl.BlockSpec((HALO, width), prev_map), pl.BlockSpec((ts, width), cur_map),
            pl.BlockSpec((HALO, width), next_map)]


def _conformer_kernel(ap_ref, a_ref, an_ref, gp_ref, g_ref, gn_ref, z_ref, w_ref, b_ref, lng_ref,
                      lnb_ref, o_ref, ext_ref, conv_ref, *, ts):
    i = pl.program_id(1)
    n_i = pl.num_programs(1)

    def glu(a, g):
        return a.astype(F32) * (1.0 / (1.0 + jnp.exp(-g.astype(F32))))

    _fill_ext(ext_ref, glu(ap_ref[...], gp_ref[...]), glu(a_ref[...], g_ref[...]),
              glu(an_ref[...], gn_ref[...]), i, n_i, ts)

    def emit(r0, sl, acc):
        conv_ref[pl.ds(r0, acc.shape[0]), sl] = acc + b_ref[:, sl]

    _dwconv(ext_ref, w_ref, CONV_K, ts, CONV_W, emit)

    rows = 128

    def ln_body(rc, carry):
        r0 = pl.multiple_of(rc * rows, rows)
        v = conv_ref[pl.ds(r0, rows), :]
        vc = v - jnp.mean(v, axis=-1, keepdims=True)
        var = jnp.mean(vc * vc, axis=-1, keepdims=True)
        y = vc * lax.rsqrt(var + EPS) * lng_ref[...] + lnb_ref[...]
        y = _silu(y) * _silu(z_ref[pl.ds(r0, rows), :].astype(F32))
        o_ref[pl.ds(r0, rows), :] = y.astype(o_ref.dtype)
        return carry

    lax.fori_loop(0, ts // rows, ln_body, 0)


def _conformer(u2, w, b, lng, lnb, batch, seq, ts):
    n = batch * seq
    nblk = seq // ts
    a_specs = _halo_specs(ts, CONV_W, lambda: 1, nblk, batch)
    g_specs = _halo_specs(ts, CONV_W, lambda: 2, nblk, batch)
    small = lambda arr: pl.BlockSpec(arr.shape, lambda b_, i: (0, 0))
    return pl.pallas_call(
        functools.partial(_conformer_kernel, ts=ts),
        grid=(batch, nblk),
        in_specs=a_specs + g_specs + [
            pl.BlockSpec((ts, CONV_W), lambda b_, i: (b_ * nblk + i, 3)),
            small(w), small(b), small(lng), small(lnb)],
        out_specs=pl.BlockSpec((ts, CONV_W), lambda b_, i: (b_ * nblk + i, 0)),
        out_shape=jax.ShapeDtypeStruct((n, CONV_W), BF16),
        scratch_shapes=[pltpu.VMEM((ts + 2 * HALO, CONV_W), F32), pltpu.VMEM((ts, CONV_W), F32)],
        compiler_params=_cparams(("parallel", "parallel")),
        name="conformer_conv",
    )(u2, u2, u2, u2, u2, u2, u2, w, b, lng, lnb)


def _out_even_kernel(x_ref, ot_ref, z_ref, oa_ref, w1_ref, w2_ref, o_ref):
    ob = ot_ref[...].astype(F32).T * _silu(z_ref[...].astype(F32))
    acc = jnp.dot(ob.astype(BF16), w1_ref[...], preferred_element_type=F32)
    acc = acc + jnp.dot(oa_ref[...], w2_ref[...], preferred_element_type=F32)
    o_ref[...] = x_ref[...] + acc


def _out_even(x2d, ot, u2, oa, w1, w2, batch, seq, tm):
    n = batch * seq
    nblk = seq // tm
    return pl.pallas_call(
        _out_even_kernel,
        grid=(batch, nblk),
        in_specs=[
            pl.BlockSpec((tm, D_MODEL), lambda b, i: (b * nblk + i, 0)),
            pl.BlockSpec((None, MLA_W, tm), lambda b, i: (b, 0, i)),
            pl.BlockSpec((tm, MLA_W), lambda b, i: (b * nblk + i, 0)),
            pl.BlockSpec((tm, CONV_W), lambda b, i: (b * nblk + i, 0)),
            pl.BlockSpec(w1.shape, lambda b, i: (0, 0)),
            pl.BlockSpec(w2.shape, lambda b, i: (0, 0)),
        ],
        out_specs=pl.BlockSpec((tm, D_MODEL), lambda b, i: (b * nblk + i, 0)),
        out_shape=jax.ShapeDtypeStruct((n, D_MODEL), F32),
        compiler_params=_cparams(("parallel", "parallel")),
        name="out_proj_even",
    )(x2d, ot, u2, oa, w1, w2)


def _dt_kernel(x_ref, g_ref, w_ref, o_ref):
    xn = _rms(x_ref[...], g_ref[...]).astype(BF16)
    o_ref[...] = lax.dot_general(w_ref[...], xn, (((1,), (1,)), ((), ())), preferred_element_type=F32)


def _dt_proj(x2d, g, wdt_t, tm):
    n = x2d.shape[0]
    rows = wdt_t.shape[0]
    return pl.pallas_call(
        _dt_kernel,
        grid=(n // tm,),
        in_specs=[pl.BlockSpec((tm, D_MODEL), lambda i: (i, 0)),
                  pl.BlockSpec((1, D_MODEL), lambda i: (0, 0)),
                  pl.BlockSpec(wdt_t.shape, lambda i: (0, 0))],
        out_specs=pl.BlockSpec((rows, tm), lambda i: (0, i)),
        out_shape=jax.ShapeDtypeStruct((rows, n), F32),
        compiler_params=_cparams(("parallel",)),
        name="dt_proj",
    )(x2d, g, wdt_t)


def _xbc_conv_kernel(p_ref, c_ref, n_ref, w_ref, b_ref, o_ref, ext_ref, *, ts, width):
    i = pl.program_id(1)
    _fill_ext(ext_ref, p_ref[...].astype(F32), c_ref[...].astype(F32), n_ref[...].astype(F32),
              i, pl.num_programs(1), ts)

    def emit(r0, sl, acc):
        o_ref[pl.ds(r0, acc.shape[0]), sl] = _silu(acc + b_ref[:, sl]).astype(o_ref.dtype)

    _dwconv(ext_ref, w_ref, SSD_CONV_K, ts, width, emit)


def _xbc_conv(uo, w, b, batch, seq, ts, width, col0):
    n = batch * seq
    nblk = seq // ts
    ncol = XBC_W // width
    specs = _halo_specs(ts, width, lambda j: col0 + j, nblk, batch)
    return pl.pallas_call(
        functools.partial(_xbc_conv_kernel, ts=ts, width=width),
        grid=(batch, nblk, ncol),
        in_specs=specs + [pl.BlockSpec((SSD_CONV_K, width), lambda b_, i, j: (0, j)),
                          pl.BlockSpec((1, width), lambda b_, i, j: (0, j))],
        out_specs=pl.BlockSpec((ts, width), lambda b_, i, j: (b_ * nblk + i, j)),
        out_shape=jax.ShapeDtypeStruct((n, XBC_W), BF16),
        scratch_shapes=[pltpu.VMEM((ts + 2 * HALO, width), F32)],
        compiler_params=_cparams(("parallel", "parallel", "parallel")),
        name="xbc_conv",
    )(uo, uo, uo, w, b)


def _ssd_kernel(xbc_ref, dt_ref, bias_ref, alog_ref, y_ref, state_ref, lhs_ref, btw_ref, *, cps):
    d = pl.program_id(1)
    c = pl.program_id(2)
    L = SSD_CHUNK
    G = SSD_GROUPS
    pairs = SSD_HPG // 2
    gw = SSD_HPG * SSD_HEAD_DIM

    @pl.when(c == 0)
    def _():
        state_ref[...] = jnp.zeros(state_ref.shape, F32)

    row = lax.broadcasted_iota(jnp.int32, (L, L), 0)
    col = lax.broadcasted_iota(jnp.int32, (L, L), 1)
    lane_lo = col < SSD_HEAD_DIM
    ahead = (row - col) * (1 - 2 * d)
    mask = ahead >= 0
    tri = jnp.where(ahead <= 0, 1.0, 0.0)
    stack_rows = lambda ref: jnp.concatenate([ref[g] for g in range(G)], axis=0)
    bias = stack_rows(bias_ref)
    neg_a = -math.log2(math.e) * jnp.exp(stack_rows(alog_ref))
    dt_all = stack_rows(dt_ref)

    def operands(k, pos):
        rows = pl.ds(pl.multiple_of(pos * L, L), L)
        dt_raw = dt_all[:, 0:L]
        for q in range(1, cps):
            dt_raw = jnp.where(pos == q, dt_all[:, q * L:(q + 1) * L], dt_raw)
        dtv = _softplus(dt_raw + bias)
        da = dtv * neg_a
        cs_t = jnp.dot(da, tri, preferred_element_type=F32, precision=lax.Precision.HIGHEST)
        tot = jnp.sum(da, axis=1, keepdims=True)
        etot = jnp.exp2(jnp.broadcast_to(tot, cs_t.shape))
        w_t = jnp.exp2(tot - cs_t) * dtv
        srow_t = cs_t - jnp.log2(dtv)
        cs_cols = jnp.concatenate([cs_t, jnp.zeros((L - 8 * G, L), F32)], axis=0).T
        for g in range(G):
            bm = xbc_ref[rows, SSD_W + g * SSD_STATE:SSD_W + (g + 1) * SSD_STATE]
            cm = xbc_ref[rows, SSD_W + (G + g) * SSD_STATE:SSD_W + (G + g + 1) * SSD_STATE]
            cb = lax.dot_general(cm, bm, (((1,), (1,)), ((), ())), preferred_element_type=F32)
            bt = bm.astype(F32).T
            cmf = cm.astype(F32)
            for rr in range(SSD_HPG):
                r = 8 * g + rr
                i = g * SSD_HPG + rr
                cs_col = jnp.broadcast_to(cs_cols[:, r:r + 1], (L, L))
                m = jnp.where(mask, jnp.exp2(cs_col - srow_t[r:r + 1, :]), 0.0) * cb
                lhs_ref[k, i, :, 0:L] = m.astype(BF16)
                lhs_ref[k, i, :, L:2 * L] = (cmf * jnp.exp2(cs_col)).astype(BF16)
                btw_ref[k, i] = (bt * w_t[r:r + 1, :]).astype(BF16)
        return etot

    def outputs(k, pos, etot):
        rows = pl.ds(pl.multiple_of(pos * L, L), L)
        for g in range(G):
            for j in range(pairs):
                lanes = slice(g * gw + j * LANES, g * gw + (j + 1) * LANES)
                xs_pair = xbc_ref[rows, lanes]
                st_pair = state_ref[g * pairs + j]
                rhs = jnp.concatenate([xs_pair, st_pair.astype(BF16)], axis=0)
                i0 = g * SSD_HPG + 2 * j
                ys = [jnp.dot(lhs_ref[k, i0 + u], rhs, preferred_element_type=F32) for u in range(2)]
                news = [jnp.dot(btw_ref[k, i0 + u], xs_pair, preferred_element_type=F32) for u in range(2)]
                y_ref[rows, lanes] = jnp.where(lane_lo, ys[0], ys[1]).astype(y_ref.dtype)
                r0 = 8 * g + 2 * j
                decay = jnp.where(lane_lo[:1], etot[r0:r0 + 1, :], etot[r0 + 1:r0 + 2, :])
                state_ref[g * pairs + j] = st_pair * decay + jnp.where(lane_lo, news[0], news[1])

    positions = [jnp.where(d == 0, k, cps - 1 - k) for k in range(cps)]
    decays = [operands(k, positions[k]) for k in range(cps)]
    for k in range(cps):
        outputs(k, positions[k], decays[k])


def _ssd(xbc, dt_t, bias, alog, batch, seq):
    n = batch * seq
    nc = seq // SSD_CHUNK
    cps = SSD_CHUNKS_PER_STEP if nc % SSD_CHUNKS_PER_STEP == 0 else 1
    nsteps = nc // cps
    rows = cps * SSD_CHUNK

    def tpos(b, d, c):
        return b * nsteps + jnp.where(d == 0, c, nsteps - 1 - c)

    head_rows = (None, SSD_GROUPS, 8, SSD_CHUNK)
    return pl.pallas_call(
        functools.partial(_ssd_kernel, cps=cps),
        grid=(batch, 2, nsteps),
        in_specs=[
            pl.BlockSpec((rows, XBC_W), lambda b, d, c: (tpos(b, d, c), 0)),
            pl.BlockSpec((None, SSD_GROUPS, 8, rows), lambda b, d, c: (d, 0, 0, tpos(b, d, c))),
            pl.BlockSpec(head_rows, lambda b, d, c: (d, 0, 0, 0)),
            pl.BlockSpec(head_rows, lambda b, d, c: (d, 0, 0, 0)),
        ],
        out_specs=pl.BlockSpec((None, rows, SSD_W), lambda b, d, c: (d, tpos(b, d, c), 0)),
        out_shape=jax.ShapeDtypeStruct((2, n, SSD_W), BF16),
        scratch_shapes=[pltpu.VMEM((SSD_GROUPS * SSD_HPG // 2, SSD_STATE, LANES), F32),
                        pltpu.VMEM((cps, SSD_HEADS, SSD_CHUNK, 2 * SSD_CHUNK), BF16),
                        pltpu.VMEM((cps, SSD_HEADS, SSD_STATE, SSD_CHUNK), BF16)],
        compiler_params=_cparams(("parallel", "arbitrary", "arbitrary")),
        name="ssd_scan",
    )(xbc, dt_t, bias, alog)


def _ssd_gate_kernel(y_ref, xs_ref, z_ref, dskip_ref, g_ref, o_ref):
    gw = SSD_HPG * SSD_HEAD_DIM
    y = y_ref[0].astype(F32) + y_ref[1].astype(F32) + dskip_ref[...] * xs_ref[...].astype(F32)
    y = y * _silu(z_ref[...].astype(F32))
    for g in range(SSD_GROUPS):
        sl = slice(g * gw, (g + 1) * gw)
        o_ref[:, sl] = _rms(y[:, sl], g_ref[:, sl]).astype(o_ref.dtype)


def _ssd_gate(y2, xbc, uo, dskip, g, tm):
    n = xbc.shape[0]
    return pl.pallas_call(
        _ssd_gate_kernel,
        grid=(n // tm,),
        in_specs=[
            pl.BlockSpec((2, tm, SSD_W), lambda i: (0, i, 0)),
            pl.BlockSpec((tm, SSD_W), lambda i: (i, 0)),
            pl.BlockSpec((tm, SSD_W), lambda i: (i, 0)),
            pl.BlockSpec((1, SSD_W), lambda i: (0, 0)),
            pl.BlockSpec((1, SSD_W), lambda i: (0, 0)),
        ],
        out_specs=pl.BlockSpec((tm, SSD_W), lambda i: (i, 0)),
        out_shape=jax.ShapeDtypeStruct((n, SSD_W), BF16),
        compiler_params=_cparams(("parallel",)),
        name="ssd_gate_norm",
    )(y2, xbc, uo, dskip, g)


def _short_conv_kernel(gcp_ref, gc_ref, gcn_ref, hp_ref, h_ref, hn_ref, gb_ref, z_ref, w_ref,
                       o_ref, ext_ref, *, ts):
    i = pl.program_id(1)
    mul = lambda a, b: a.astype(F32) * b.astype(F32)
    _fill_ext(ext_ref, mul(gcp_ref[...], hp_ref[...]), mul(gc_ref[...], h_ref[...]),
              mul(gcn_ref[...], hn_ref[...]), i, pl.num_programs(1), ts)

    def emit(r0, sl, acc):
        rs = pl.ds(r0, acc.shape[0])
        y = gb_ref[rs, sl].astype(F32) * acc * _silu(z_ref[rs, sl].astype(F32))
        o_ref[rs, sl] = y.astype(o_ref.dtype)

    _dwconv(ext_ref, w_ref, SC_K, ts, SC_W, emit)


def _short_conv(uo, w, batch, seq, ts, col0):
    n = batch * seq
    nblk = seq // ts
    gc_specs = _halo_specs(ts, SC_W, lambda: col0 + 1, nblk, batch)
    h_specs = _halo_specs(ts, SC_W, lambda: col0 + 2, nblk, batch)
    return pl.pallas_call(
        functools.partial(_short_conv_kernel, ts=ts),
        grid=(batch, nblk),
        in_specs=gc_specs + h_specs + [
            pl.BlockSpec((ts, SC_W), lambda b, i: (b * nblk + i, col0)),
            pl.BlockSpec((ts, SC_W), lambda b, i: (b * nblk + i, col0 + 3)),
            pl.BlockSpec(w.shape, lambda b, i: (0, 0))],
        out_specs=pl.BlockSpec((ts, SC_W), lambda b, i: (b * nblk + i, 0)),
        out_shape=jax.ShapeDtypeStruct((n, SC_W), BF16),
        scratch_shapes=[pltpu.VMEM((ts + 2 * HALO, SC_W), F32)],
        compiler_params=_cparams(("parallel", "parallel")),
        name="short_conv",
    )(uo, uo, uo, uo, uo, uo, uo, uo, w)


def _out_odd_kernel(x_ref, oc_ref, od_ref, w1_ref, w2_ref, *rest):
    o_ref = rest[-1]
    acc = jnp.dot(oc_ref[...], w1_ref[...], preferred_element_type=F32)
    acc = acc + jnp.dot(od_ref[...], w2_ref[...], preferred_element_type=F32)
    y = x_ref[...] + acc
    if len(rest) == 2:
        y = _rms(y, rest[0][...])
    o_ref[...] = y


def _out_odd(x2d, oc, od, w1, w2, final_g, tm):
    n = x2d.shape[0]
    extra = [] if final_g is None else [final_g.reshape(1, -1).astype(F32)]
    return pl.pallas_call(
        _out_odd_kernel,
        grid=(n // tm,),
        in_specs=[
            pl.BlockSpec((tm, D_MODEL), lambda i: (i, 0)),
            pl.BlockSpec((tm, SSD_W), lambda i: (i, 0)),
            pl.BlockSpec((tm, SC_W), lambda i: (i, 0)),
            pl.BlockSpec(w1.shape, lambda i: (0, 0)),
            pl.BlockSpec(w2.shape, lambda i: (0, 0)),
        ] + [pl.BlockSpec((1, D_MODEL), lambda i: (0, 0)) for _ in extra],
        out_specs=pl.BlockSpec((tm, D_MODEL), lambda i: (i, 0)),
        out_shape=jax.ShapeDtypeStruct((n, D_MODEL), F32),
        compiler_params=_cparams(("parallel",)),
        name="out_proj_odd",
    )(x2d, oc, od, w1, w2, *extra)


def _final_norm_kernel(x_ref, g_ref, o_ref):
    o_ref[...] = _rms(x_ref[...], g_ref[...])


def _final_norm(x2d, g, tm):
    n = x2d.shape[0]
    return pl.pallas_call(
        _final_norm_kernel,
        grid=(n // tm,),
        in_specs=[pl.BlockSpec((tm, D_MODEL), lambda i: (i, 0)), pl.BlockSpec((1, D_MODEL), lambda i: (0, 0))],
        out_specs=pl.BlockSpec((tm, D_MODEL), lambda i: (i, 0)),
        out_shape=jax.ShapeDtypeStruct((n, D_MODEL), F32),
        compiler_params=_cparams(("parallel",)),
        name="final_norm",
    )(x2d, g)


def _tile(total, want):
    return want if total % want == 0 else total


def _even_layer(x2d, batch, seq, norm_g, w_in, w_uq, q_g, w_ukv, kv_g, conv_w, conv_b, ln_g, ln_b, w_out,
                tables):
    n = batch * seq
    low = Q_LORA + KV_LORA + QK_ROPE
    row = lambda v: v.reshape(1, -1).astype(F32)

    wa = jnp.pad(w_in[:, :low], ((0, 0), (0, 6 * LANES - low))).astype(BF16)
    w_rest = w_in[:, low:].astype(BF16)
    wq = w_uq.reshape(Q_LORA, MLA_HEADS, QK_NOPE + QK_ROPE)
    wqt = jnp.pad(wq, ((0, 0), (0, 0), (0, QK_PAD - QK_NOPE - QK_ROPE)))
    wqt = wqt.reshape(Q_LORA, MLA_HEADS * QK_PAD).T.astype(BF16)
    wkv = w_ukv.reshape(KV_LORA, MLA_HEADS, QK_NOPE + V_DIM)
    wk = jnp.pad(wkv[:, :, :QK_NOPE], ((0, 0), (0, 0), (0, QK_PAD - QK_NOPE)))
    wk = wk.reshape(KV_LORA, MLA_HEADS * QK_PAD).astype(BF16)
    wvt = jnp.pad(wkv[:, :, QK_NOPE:], ((0, 0), (0, 0), (0, VT_ROWS - V_DIM)))
    wvt = wvt.reshape(KV_LORA, MLA_HEADS * VT_ROWS).T.astype(BF16)
    cols = jnp.arange(MLA_HEADS * QK_PAD)
    place = ((cols[None, :] % QK_PAD) == (jnp.arange(LANES)[:, None] + QK_NOPE))
    place = (place & (jnp.arange(LANES)[:, None] < QK_ROPE)).astype(BF16)
    ones_col = ((jnp.arange(MLA_HEADS * VT_ROWS) % VT_ROWS) == V_DIM).astype(F32)[:, None]

    tm = _tile(seq, TOKEN_TILE)
    u2 = _rms_matmul(x2d, row(norm_g), w_rest, _tile(n, PROJ_ROWS), _tile(w_rest.shape[1], PROJ_COLS),
                     BF16)
    qt, k, vt = _mla_prep(x2d, row(norm_g), wa, row(q_g), row(kv_g), wqt, wk, place, wvt, ones_col,
                          *tables, batch, seq, tm)
    ot = _attention(qt, k, vt, batch, seq, _tile(seq, QUERY_TILE), tm)
    oa = _conformer(u2, conv_w.astype(F32), row(conv_b), row(ln_g), row(ln_b), batch, seq,
                    _tile(seq, CONV_TILE))
    w_out = w_out.astype(BF16)
    return _out_even(x2d, ot, u2, oa, w_out[:MLA_W], w_out[MLA_W:], batch, seq, _tile(seq, OUT_ROWS))


def _odd_layer(x2d, batch, seq, norm_g, w_in, conv_c_w, conv_c_b, dt_bias_f, dt_bias_b, a_log_f, a_log_b,
               d_skip, ssd_g, conv_d_w, w_out, final_g):
    n = batch * seq
    row = lambda v: v.reshape(1, -1).astype(F32)
    o_dt = SSD_W + XBC_W
    w_main = jnp.concatenate([w_in[:, :o_dt], w_in[:, o_dt + 2 * SSD_HEADS:]], axis=1).astype(BF16)
    w_dt = w_in[:, o_dt:o_dt + 2 * SSD_HEADS].reshape(D_MODEL, 2 * SSD_GROUPS, SSD_HPG)
    wdt_t = jnp.pad(w_dt, ((0, 0), (0, 0), (0, 8 - SSD_HPG))).reshape(D_MODEL, 64).T.astype(BF16)

    def per_head(vf, vb):
        v = jnp.concatenate([vf, vb]).reshape(2, SSD_GROUPS, SSD_HPG)
        v = jnp.pad(v, ((0, 0), (0, 0), (0, 8 - SSD_HPG)))
        return jnp.broadcast_to(v[..., None], (2, SSD_GROUPS, 8, SSD_CHUNK)).astype(F32)

    tm = _tile(seq, TOKEN_TILE)
    uo = _rms_matmul(x2d, row(norm_g), w_main, _tile(n, PROJ_ROWS), _tile(w_main.shape[1], PROJ_COLS),
                     BF16)
    dt_t = _dt_proj(x2d, row(norm_g), wdt_t, _tile(n, PROJ_ROWS)).reshape(2, SSD_GROUPS, 8, n)
    xbc = _xbc_conv(uo, conv_c_w.astype(F32), row(conv_c_b), batch, seq, _tile(seq, CONV_TILE), 512,
                    SSD_W // 512)
    y2 = _ssd(xbc, dt_t, per_head(dt_bias_f, dt_bias_b), per_head(a_log_f, a_log_b), batch, seq)
    dskip = row(jnp.repeat(d_skip, SSD_HEAD_DIM))
    oc = _ssd_gate(y2, xbc, uo, dskip, row(ssd_g), tm)
    od = _short_conv(uo, conv_d_w.astype(F32), batch, seq, _tile(seq, CONV_TILE), (SSD_W + XBC_W) // SC_W)
    w_out = w_out.astype(BF16)
    return _out_odd(x2d, oc, od, w_out[:SSD_W], w_out[SSD_W:], final_g, _tile(n, OUT_ROWS))


def _rope_tables(seq):
    half = QK_ROPE // 2
    inv = ROPE_THETA ** (-jnp.arange(half, dtype=F32) / half)
    ang = jnp.arange(seq, dtype=F32)[:, None] * inv[None, :]
    cos, sin = jnp.cos(ang), jnp.sin(ang)
    pad = ((0, 0), (0, LANES - QK_ROPE))
    cos2 = jnp.pad(jnp.concatenate([cos, cos], axis=1), pad)
    sin2 = jnp.pad(jnp.concatenate([-sin, sin], axis=1), pad)
    return cos2, sin2, cos.T, sin.T


def kernel(x, norm_e, w_in_e, w_uq, q_norm, w_ukv, kv_norm, conv_a_w, conv_a_b, ln_a_g, ln_a_b, w_out_e, norm_o, w_in_o, conv_c_w, conv_c_b, dt_bias_f, dt_bias_b, a_log_f, a_log_b, d_skip, ssd_norm, conv_d_w, w_out_o, final_norm):
    batch, seq, _ = x.shape
    depth = norm_e.shape[0] + norm_o.shape[0]
    tables = _rope_tables(seq)
    x2d = x.reshape(batch * seq, D_MODEL)
    for i in range(depth):
        j = i // 2
        if i % 2 == 0:
            x2d = _even_layer(x2d, batch, seq, norm_e[j], w_in_e[j], w_uq[j], q_norm[j], w_ukv[j], kv_norm[j],
                              conv_a_w[j], conv_a_b[j], ln_a_g[j], ln_a_b[j], w_out_e[j], tables)
        else:
            last = i == depth - 1
            x2d = _odd_layer(x2d, batch, seq, norm_o[j], w_in_o[j], conv_c_w[j], conv_c_b[j], dt_bias_f[j],
                             dt_bias_b[j], a_log_f[j], a_log_b[j], d_skip[j], ssd_norm[j], conv_d_w[j],
                             w_out_o[j], final_norm if last else None)
    if depth % 2 == 1:
        x2d = _final_norm(x2d, final_norm.reshape(1, -1), _tile(batch * seq, TOKEN_TILE))
    return x2d.reshape(batch, seq, D_MODEL)
```

```python
import functools
import math

import jax
import jax.numpy as jnp
from jax import lax
from jax.experimental import pallas as pl
from jax.experimental.pallas import tpu as pltpu

F32 = jnp.float32
BF16 = jnp.bfloat16
EPS = 1e-6

D_MODEL = 1024
MLA_HEADS = 16
QK_NOPE = 64
QK_ROPE = 32
V_DIM = 64
Q_LORA = 384
KV_LORA = 256
MLA_W = MLA_HEADS * V_DIM
ROPE_THETA = 10000.0
CONV_W = 1024
CONV_K = 31
SSD_HEAD_DIM = 64
SSD_HEADS = 24
SSD_W = SSD_HEADS * SSD_HEAD_DIM
SSD_GROUPS = 4
SSD_HPG = SSD_HEADS // SSD_GROUPS
SSD_STATE = 128
SSD_CONV_K = 5
SSD_CHUNK = 128
XBC_W = SSD_W + 2 * SSD_GROUPS * SSD_STATE
SC_W = 512
SC_K = 3

LANES = 128
HALO = 16
QK_PAD = 128
VT_ROWS = 80
ATTN_SAFE_MAX = 1e30
VMEM_LIMIT = 56 * 1024 * 1024

TOKEN_TILE = 512
CONV_TILE = 1024
QUERY_TILE = 2048
PROJ_ROWS = 1024
PROJ_COLS = 2048
OUT_ROWS = 1024
SSD_CHUNKS_PER_STEP = 4


def _cparams(sem):
    return pltpu.CompilerParams(dimension_semantics=sem, vmem_limit_bytes=VMEM_LIMIT)


def _silu(v):
    return v * (1.0 / (1.0 + jnp.exp(-v)))


def _softplus(v):
    return jnp.maximum(v, 0.0) + jnp.log(1.0 + jnp.exp(-jnp.abs(v)))


def _rms(v, g):
    ms = jnp.mean(v * v, axis=-1, keepdims=True)
    return v * lax.rsqrt(ms + EPS) * g


def _rms_matmul_kernel(x_ref, g_ref, w_ref, *rest):
    o_ref, xn_ref = rest[-3 if len(rest) == 4 else -2], rest[-1]

    @pl.when(pl.program_id(1) == 0)
    def _():
        xn_ref[...] = _rms(x_ref[...], g_ref[...]).astype(xn_ref.dtype)
        if len(rest) == 4:
            rest[2][...] = lax.dot_general(rest[0][...], xn_ref[...], (((1,), (1,)), ((), ())),
                                           preferred_element_type=F32)

    o_ref[...] = jnp.dot(xn_ref[...], w_ref[...], preferred_element_type=F32).astype(o_ref.dtype)


def _rms_matmul(x, g, w, tm, tn, out_dtype, wt=None):
    n, d = x.shape
    e = w.shape[1]
    in_specs = [
        pl.BlockSpec((tm, d), lambda i, j: (i, 0)),
        pl.BlockSpec((1, d), lambda i, j: (0, 0)),
        pl.BlockSpec((d, tn), lambda i, j: (0, j)),
    ]
    out_specs = [pl.BlockSpec((tm, tn), lambda i, j: (i, j))]
    out_shape = [jax.ShapeDtypeStruct((n, e), out_dtype)]
    args = [x, g, w]
    if wt is not None:
        in_specs.append(pl.BlockSpec(wt.shape, lambda i, j: (0, 0)))
        out_specs.append(pl.BlockSpec((wt.shape[0], tm), lambda i, j: (0, i)))
        out_shape.append(jax.ShapeDtypeStruct((wt.shape[0], n), F32))
        args.append(wt)
    outs = pl.pallas_call(
        _rms_matmul_kernel,
        grid=(n // tm, e // tn),
        in_specs=in_specs,
        out_specs=out_specs,
        out_shape=out_shape,
        scratch_shapes=[pltpu.VMEM((tm, d), BF16)],
        compiler_params=_cparams(("parallel", "arbitrary")),
        name="rms_matmul",
    )(*args)
    return outs[0] if wt is None else outs


def _mla_prep_kernel(x_ref, g_ref, wa_ref, qg_ref, kvg_ref, wqt_ref, wk_ref, place_ref, wvt_ref,
                     ones_ref, cos2_ref, sin2_ref, cost_ref, sint_ref,
                     qt_ref, k_ref, vt_ref, *, qscale):
    xn = _rms(x_ref[...], g_ref[...]).astype(BF16)
    a = jnp.dot(xn, wa_ref[...], preferred_element_type=F32)
    qn = _rms(a[:, :Q_LORA], qg_ref[...]).astype(BF16)
    kvn = _rms(a[:, Q_LORA:Q_LORA + KV_LORA], kvg_ref[...]).astype(BF16)

    kr = a[:, Q_LORA + KV_LORA:]
    lane = lax.broadcasted_iota(jnp.int32, kr.shape, 1)
    half = QK_ROPE // 2
    swapped = jnp.where(lane < half, pltpu.roll(kr, LANES - half, 1), pltpu.roll(kr, half, 1))
    kr = kr * cos2_ref[...] + swapped * sin2_ref[...]

    k = jnp.dot(kvn, wk_ref[...], preferred_element_type=F32)
    k = k + jnp.dot(kr.astype(BF16), place_ref[...], preferred_element_type=F32)
    k_ref[...] = k.astype(k_ref.dtype)

    nt = (((1,), (1,)), ((), ()))
    vt = lax.dot_general(wvt_ref[...], kvn, nt, preferred_element_type=F32) + ones_ref[...]
    vt_ref[...] = vt.astype(vt_ref.dtype)

    qt = lax.dot_general(wqt_ref[...], qn, nt, preferred_element_type=F32) * qscale
    qt_ref[...] = qt.astype(qt_ref.dtype)
    cos_t = cost_ref[...]
    sin_t = sint_ref[...]
    for h in range(MLA_HEADS):
        r0 = h * QK_PAD + QK_NOPE
        x1 = qt[r0:r0 + half]
        x2 = qt[r0 + half:r0 + 2 * half]
        qt_ref[r0:r0 + half, :] = (x1 * cos_t - x2 * sin_t).astype(qt_ref.dtype)
        qt_ref[r0 + half:r0 + 2 * half, :] = (x2 * cos_t + x1 * sin_t).astype(qt_ref.dtype)


def _mla_prep(x2d, g, wa, qg, kvg, wqt, wk, place, wvt, ones_col, cos2, sin2, cos_t, sin_t,
              batch, seq, tm):
    nblk = seq // tm
    qscale = (QK_NOPE + QK_ROPE) ** -0.5 * math.log2(math.e)
    full = lambda arr: pl.BlockSpec(arr.shape, lambda b, i: (0,) * arr.ndim)
    hq = MLA_HEADS * QK_PAD
    hv = MLA_HEADS * VT_ROWS
    return pl.pallas_call(
        functools.partial(_mla_prep_kernel, qscale=qscale),
        grid=(batch, nblk),
        in_specs=[
            pl.BlockSpec((tm, D_MODEL), lambda b, i: (b * nblk + i, 0)),
            full(g), full(wa), full(qg), full(kvg), full(wqt), full(wk), full(place), full(wvt),
            full(ones_col),
            pl.BlockSpec((tm, LANES), lambda b, i: (i, 0)),
            pl.BlockSpec((tm, LANES), lambda b, i: (i, 0)),
            pl.BlockSpec((QK_ROPE // 2, tm), lambda b, i: (0, i)),
            pl.BlockSpec((QK_ROPE // 2, tm), lambda b, i: (0, i)),
        ],
        out_specs=[
            pl.BlockSpec((None, hq, tm), lambda b, i: (b, 0, i)),
            pl.BlockSpec((None, tm, hq), lambda b, i: (b, i, 0)),
            pl.BlockSpec((None, None, hv, tm), lambda b, i: (b, i, 0, 0)),
        ],
        out_shape=[
            jax.ShapeDtypeStruct((batch, hq, seq), BF16),
            jax.ShapeDtypeStruct((batch, seq, hq), BF16),
            jax.ShapeDtypeStruct((batch, nblk, hv, tm), BF16),
        ],
        compiler_params=_cparams(("parallel", "parallel")),
        name="mla_prep",
    )(x2d, g, wa, qg, kvg, wqt, wk, place, wvt, ones_col, cos2, sin2, cos_t, sin_t)


def _attn_kernel(qt_ref, qn_ref, k_ref, vt_ref, o_ref, m_ref, acc_ref, s0_ref, c0_ref, *, tk, nk):
    def first_scores(q_ref):
        st = jnp.dot(k_ref[0:tk, :], q_ref[...], preferred_element_type=F32)
        s0_ref[...] = st
        c0_ref[...] = jnp.max(st, axis=0, keepdims=True)

    @pl.when(pl.program_id(2) == 0)
    def _():
        first_scores(qt_ref)

    qt = qt_ref[...]
    m0 = c0_ref[...]
    acc = jnp.dot(vt_ref[0], jnp.exp2(s0_ref[...] - m0).astype(BF16), preferred_element_type=F32)
    for c in range(1, nk):
        st = jnp.dot(k_ref[c * tk:(c + 1) * tk, :], qt, preferred_element_type=F32)
        acc = acc + jnp.dot(vt_ref[c], jnp.exp2(st - m0).astype(BF16), preferred_element_type=F32)
    acc_ref[...] = acc
    first_scores(qn_ref)

    used = acc[:V_DIM + 8]
    unsafe = jnp.max(jnp.where(jnp.abs(used) < ATTN_SAFE_MAX, 0.0, 1.0)) > 0.0

    @pl.when(unsafe)
    def _():
        m_ref[...] = jnp.full(m_ref.shape, -jnp.inf, F32)
        acc_ref[...] = jnp.zeros(acc_ref.shape, F32)

        def body(c, carry):
            k0 = pl.multiple_of(c * tk, tk)
            st = jnp.dot(k_ref[pl.ds(k0, tk), :], qt, preferred_element_type=F32)
            m_old = m_ref[...]
            m_new = jnp.maximum(m_old, jnp.max(st, axis=0, keepdims=True))
            p = jnp.exp2(st - m_new).astype(BF16)
            pv = jnp.dot(vt_ref[c], p, preferred_element_type=F32)
            acc_ref[...] = acc_ref[...] * jnp.exp2(m_old - m_new) + pv
            m_ref[...] = m_new
            return carry

        lax.fori_loop(0, nk, body, 0)

    acc = acc_ref[...]
    o_ref[...] = (acc[:V_DIM] * (1.0 / acc[V_DIM:V_DIM + 1])).astype(o_ref.dtype)


def _attention(qt, k, vt, batch, seq, tq, tk):
    nk = seq // tk
    nq = seq // tq
    return pl.pallas_call(
        functools.partial(_attn_kernel, tk=tk, nk=nk),
        grid=(batch, MLA_HEADS, nq),
        in_specs=[
            pl.BlockSpec((None, QK_PAD, tq), lambda b, h, i: (b, h, i)),
            pl.BlockSpec((None, QK_PAD, tq), lambda b, h, i: (b, h, jnp.minimum(i + 1, nq - 1))),
            pl.BlockSpec((None, seq, QK_PAD), lambda b, h, i: (b, 0, h)),
            pl.BlockSpec((None, nk, VT_ROWS, tk), lambda b, h, i: (b, 0, h, 0)),
        ],
        out_specs=pl.BlockSpec((None, V_DIM, tq), lambda b, h, i: (b, h, i)),
        out_shape=jax.ShapeDtypeStruct((batch, MLA_W, seq), BF16),
        scratch_shapes=[pltpu.VMEM((1, tq), F32), pltpu.VMEM((VT_ROWS, tq), F32),
                        pltpu.VMEM((tk, tq), F32), pltpu.VMEM((1, tq), F32)],
        compiler_params=_cparams(("parallel", "parallel", "arbitrary")),
        name="mla_attention",
    )(qt, qt, k, vt)


def _fill_ext(ext_ref, prev, cur, nxt, i, n_i, ts):
    ext_ref[0:HALO, :] = jnp.where(i > 0, prev, 0.0)
    ext_ref[HALO:HALO + ts, :] = cur
    ext_ref[HALO + ts:2 * HALO + ts, :] = jnp.where(i < n_i - 1, nxt, 0.0)


def _dwconv(ext_ref, w_ref, k_taps, ts, width, emit, rows=128):
    def body(rc, carry):
        r0 = pl.multiple_of(rc * rows, rows)
        for cb in range(width // LANES):
            sl = slice(cb * LANES, (cb + 1) * LANES)
            emit(r0, sl, _dwconv_block(ext_ref, w_ref, k_taps, r0, rows, sl))
        return carry

    lax.fori_loop(0, ts // rows, body, 0)


def _dwconv_block(ext_ref, w_ref, k_taps, r0, rows, sl):
    sub = 8
    off = HALO - k_taps // 2
    lo = (off // sub) * sub
    win_rows = ((off - lo + k_taps - 1) // sub + 1) * sub + rows
    win = ext_ref[pl.ds(r0 + lo, win_rows), sl]
    acc = jnp.zeros((rows, LANES), F32)
    for phase in range(sub):
        taps = [k for k in range(k_taps) if (off - lo + k) % sub == phase]
        if not taps:
            continue
        shifted = win if phase == 0 else pltpu.roll(win, win_rows - phase, 0)
        for k in taps:
            q = (off - lo + k) // sub * sub
            acc = acc + w_ref[k:k + 1, sl] * shifted[q:q + rows]
    return acc


def _halo_specs(ts, width, col, nblk, batch):
    per = ts // HALO
    last = batch * nblk * per - 1

    def prev_map(b, i, *_):
        return (jnp.maximum((b * nblk + i) * per - 1, 0), col(*_))

    def cur_map(b, i, *_):
        return (b * nblk + i, col(*_))

    def next_map(b, i, *_):
        return (jnp.minimum((b * nblk + i + 1) * per, last), col(*_))

    return [pl.BlockSpec((HALO, width), prev_map), pl.BlockSpec((ts, width), cur_map),
            pl.BlockSpec((HALO, width), next_map)]


def _conformer_kernel(ap_ref, a_ref, an_ref, gp_ref, g_ref, gn_ref, z_ref, w_ref, b_ref, lng_ref,
                      lnb_ref, o_ref, ext_ref, conv_ref, *, ts):
    i = pl.program_id(1)
    n_i = pl.num_programs(1)

    def glu(a, g):
        return a.astype(F32) * (1.0 / (1.0 + jnp.exp(-g.astype(F32))))

    _fill_ext(ext_ref, glu(ap_ref[...], gp_ref[...]), glu(a_ref[...], g_ref[...]),
              glu(an_ref[...], gn_ref[...]), i, n_i, ts)

    def emit(r0, sl, acc):
        conv_ref[pl.ds(r0, acc.shape[0]), sl] = acc + b_ref[:, sl]

    _dwconv(ext_ref, w_ref, CONV_K, ts, CONV_W, emit)

    rows = 128

    def ln_body(rc, carry):
        r0 = pl.multiple_of(rc * rows, rows)
        v = conv_ref[pl.ds(r0, rows), :]
        vc = v - jnp.mean(v, axis=-1, keepdims=True)
        var = jnp.mean(vc * vc, axis=-1, keepdims=True)
        y = vc * lax.rsqrt(var + EPS) * lng_ref[...] + lnb_ref[...]
        y = _silu(y) * _silu(z_ref[pl.ds(r0, rows), :].astype(F32))
        o_ref[pl.ds(r0, rows), :] = y.astype(o_ref.dtype)
        return carry

    lax.fori_loop(0, ts // rows, ln_body, 0)


def _conformer(u2, w, b, lng, lnb, batch, seq, ts):
    n = batch * seq
    nblk = seq // ts
    a_specs = _halo_specs(ts, CONV_W, lambda: 1, nblk, batch)
    g_specs = _halo_specs(ts, CONV_W, lambda: 2, nblk, batch)
    small = lambda arr: pl.BlockSpec(arr.shape, lambda b_, i: (0, 0))
    return pl.pallas_call(
        functools.partial(_conformer_kernel, ts=ts),
        grid=(batch, nblk),
        in_specs=a_specs + g_specs + [
            pl.BlockSpec((ts, CONV_W), lambda b_, i: (b_ * nblk + i, 3)),
            small(w), small(b), small(lng), small(lnb)],
        out_specs=pl.BlockSpec((ts, CONV_W), lambda b_, i: (b_ * nblk + i, 0)),
        out_shape=jax.ShapeDtypeStruct((n, CONV_W), BF16),
        scratch_shapes=[pltpu.VMEM((ts + 2 * HALO, CONV_W), F32), pltpu.VMEM((ts, CONV_W), F32)],
        compiler_params=_cparams(("parallel", "parallel")),
        name="conformer_conv",
    )(u2, u2, u2, u2, u2, u2, u2, w, b, lng, lnb)


def _out_even_kernel(x_ref, ot_ref, z_ref, oa_ref, w1_ref, w2_ref, o_ref):
    ob = ot_ref[...].astype(F32).T * _silu(z_ref[...].astype(F32))
    acc = jnp.dot(ob.astype(BF16), w1_ref[...], preferred_element_type=F32)
    acc = acc + jnp.dot(oa_ref[...], w2_ref[...], preferred_element_type=F32)
    o_ref[...] = x_ref[...] + acc


def _out_even(x2d, ot, u2, oa, w1, w2, batch, seq, tm):
    n = batch * seq
    nblk = seq // tm
    return pl.pallas_call(
        _out_even_kernel,
        grid=(batch, nblk),
        in_specs=[
            pl.BlockSpec((tm, D_MODEL), lambda b, i: (b * nblk + i, 0)),
            pl.BlockSpec((None, MLA_W, tm), lambda b, i: (b, 0, i)),
            pl.BlockSpec((tm, MLA_W), lambda b, i: (b * nblk + i, 0)),
            pl.BlockSpec((tm, CONV_W), lambda b, i: (b * nblk + i, 0)),
            pl.BlockSpec(w1.shape, lambda b, i: (0, 0)),
            pl.BlockSpec(w2.shape, lambda b, i: (0, 0)),
        ],
        out_specs=pl.BlockSpec((tm, D_MODEL), lambda b, i: (b * nblk + i, 0)),
        out_shape=jax.ShapeDtypeStruct((n, D_MODEL), F32),
        compiler_params=_cparams(("parallel", "parallel")),
        name="out_proj_even",
    )(x2d, ot, u2, oa, w1, w2)


def _xbc_conv_kernel(p_ref, c_ref, n_ref, w_ref, b_ref, o_ref, ext_ref, *, ts, width):
    i = pl.program_id(1)
    _fill_ext(ext_ref, p_ref[...].astype(F32), c_ref[...].astype(F32), n_ref[...].astype(F32),
              i, pl.num_programs(1), ts)

    def emit(r0, sl, acc):
        o_ref[pl.ds(r0, acc.shape[0]), sl] = _silu(acc + b_ref[:, sl]).astype(o_ref.dtype)

    _dwconv(ext_ref, w_ref, SSD_CONV_K, ts, width, emit)


def _xbc_conv(uo, w, b, batch, seq, ts, width, col0):
    n = batch * seq
    nblk = seq // ts
    ncol = XBC_W // width
    specs = _halo_specs(ts, width, lambda j: col0 + j, nblk, batch)
    return pl.pallas_call(
        functools.partial(_xbc_conv_kernel, ts=ts, width=width),
        grid=(batch, nblk, ncol),
        in_specs=specs + [pl.BlockSpec((SSD_CONV_K, width), lambda b_, i, j: (0, j)),
                          pl.BlockSpec((1, width), lambda b_, i, j: (0, j))],
        out_specs=pl.BlockSpec((ts, width), lambda b_, i, j: (b_ * nblk + i, j)),
        out_shape=jax.ShapeDtypeStruct((n, XBC_W), BF16),
        scratch_shapes=[pltpu.VMEM((ts + 2 * HALO, width), F32)],
        compiler_params=_cparams(("parallel", "parallel", "parallel")),
        name="xbc_conv",
    )(uo, uo, uo, w, b)


def _ssd_kernel(xbc_ref, dt_ref, bias_ref, alog_ref, y_ref, state_ref, lhs_ref, btw_ref, *, cps):
    d = pl.program_id(1)
    c = pl.program_id(2)
    L = SSD_CHUNK
    G = SSD_GROUPS
    pairs = SSD_HPG // 2
    gw = SSD_HPG * SSD_HEAD_DIM

    @pl.when(c == 0)
    def _():
        state_ref[...] = jnp.zeros(state_ref.shape, F32)

    row = lax.broadcasted_iota(jnp.int32, (L, L), 0)
    col = lax.broadcasted_iota(jnp.int32, (L, L), 1)
    lane_lo = col < SSD_HEAD_DIM
    ahead = (row - col) * (1 - 2 * d)
    mask = ahead >= 0
    tri = jnp.where(ahead <= 0, 1.0, 0.0)
    stack_rows = lambda ref: jnp.concatenate([ref[g] for g in range(G)], axis=0)
    bias = stack_rows(bias_ref)
    neg_a = -math.log2(math.e) * jnp.exp(stack_rows(alog_ref))
    dt_all = stack_rows(dt_ref)

    def operands(k, pos):
        rows = pl.ds(pl.multiple_of(pos * L, L), L)
        dt_raw = dt_all[:, 0:L]
        for q in range(1, cps):
            dt_raw = jnp.where(pos == q, dt_all[:, q * L:(q + 1) * L], dt_raw)
        dtv = _softplus(dt_raw + bias)
        da = dtv * neg_a
        cs_t = jnp.dot(da, tri, preferred_element_type=F32, precision=lax.Precision.HIGHEST)
        tot = jnp.sum(da, axis=1, keepdims=True)
        etot = jnp.exp2(jnp.broadcast_to(tot, cs_t.shape))
        w_t = jnp.exp2(tot - cs_t) * dtv
        srow_t = cs_t - jnp.log2(dtv)
        cs_cols = jnp.concatenate([cs_t, jnp.zeros((L - 8 * G, L), F32)], axis=0).T
        for g in range(G):
            bm = xbc_ref[rows, SSD_W + g * SSD_STATE:SSD_W + (g + 1) * SSD_STATE]
            cm = xbc_ref[rows, SSD_W + (G + g) * SSD_STATE:SSD_W + (G + g + 1) * SSD_STATE]
            cb = lax.dot_general(cm, bm, (((1,), (1,)), ((), ())), preferred_element_type=F32)
            bt = bm.astype(F32).T
            cmf = cm.astype(F32)
            for rr in range(SSD_HPG):
                r = 8 * g + rr
                i = g * SSD_HPG + rr
                cs_col = jnp.broadcast_to(cs_cols[:, r:r + 1], (L, L))
                m = jnp.where(mask, jnp.exp2(cs_col - srow_t[r:r + 1, :]), 0.0) * cb
                lhs_ref[k, i, :, 0:L] = m.astype(BF16)
                lhs_ref[k, i, :, L:2 * L] = (cmf * jnp.exp2(cs_col)).astype(BF16)
                btw_ref[k, i] = (bt * w_t[r:r + 1, :]).astype(BF16)
        return etot

    def outputs(k, pos, etot):
        rows = pl.ds(pl.multiple_of(pos * L, L), L)
        for g in range(G):
            for j in range(pairs):
                lanes = slice(g * gw + j * LANES, g * gw + (j + 1) * LANES)
                xs_pair = xbc_ref[rows, lanes]
                st_pair = state_ref[g * pairs + j]
                rhs = jnp.concatenate([xs_pair, st_pair.astype(BF16)], axis=0)
                i0 = g * SSD_HPG + 2 * j
                ys = [jnp.dot(lhs_ref[k, i0 + u], rhs, preferred_element_type=F32) for u in range(2)]
                news = [jnp.dot(btw_ref[k, i0 + u], xs_pair, preferred_element_type=F32) for u in range(2)]
                y_ref[rows, lanes] = jnp.where(lane_lo, ys[0], ys[1]).astype(y_ref.dtype)
                r0 = 8 * g + 2 * j
                decay = jnp.where(lane_lo[:1], etot[r0:r0 + 1, :], etot[r0 + 1:r0 + 2, :])
                state_ref[g * pairs + j] = st_pair * decay + jnp.where(lane_lo, news[0], news[1])

    positions = [jnp.where(d == 0, k, cps - 1 - k) for k in range(cps)]
    decays = [operands(k, positions[k]) for k in range(cps)]
    for k in range(cps):
        outputs(k, positions[k], decays[k])


def _ssd(xbc, dt_t, bias, alog, batch, seq):
    n = batch * seq
    nc = seq // SSD_CHUNK
    cps = SSD_CHUNKS_PER_STEP if nc % SSD_CHUNKS_PER_STEP == 0 else 1
    nsteps = nc // cps
    rows = cps * SSD_CHUNK

    def tpos(b, d, c):
        return b * nsteps + jnp.where(d == 0, c, nsteps - 1 - c)

    head_rows = (None, SSD_GROUPS, 8, SSD_CHUNK)
    return pl.pallas_call(
        functools.partial(_ssd_kernel, cps=cps),
        grid=(batch, 2, nsteps),
        in_specs=[
            pl.BlockSpec((rows, XBC_W), lambda b, d, c: (tpos(b, d, c), 0)),
            pl.BlockSpec((None, SSD_GROUPS, 8, rows), lambda b, d, c: (d, 0, 0, tpos(b, d, c))),
            pl.BlockSpec(head_rows, lambda b, d, c: (d, 0, 0, 0)),
            pl.BlockSpec(head_rows, lambda b, d, c: (d, 0, 0, 0)),
        ],
        out_specs=pl.BlockSpec((None, rows, SSD_W), lambda b, d, c: (d, tpos(b, d, c), 0)),
        out_shape=jax.ShapeDtypeStruct((2, n, SSD_W), BF16),
        scratch_shapes=[pltpu.VMEM((SSD_GROUPS * SSD_HPG // 2, SSD_STATE, LANES), F32),
                        pltpu.VMEM((cps, SSD_HEADS, SSD_CHUNK, 2 * SSD_CHUNK), BF16),
                        pltpu.VMEM((cps, SSD_HEADS, SSD_STATE, SSD_CHUNK), BF16)],
        compiler_params=_cparams(("parallel", "arbitrary", "arbitrary")),
        name="ssd_scan",
    )(xbc, dt_t, bias, alog)


def _short_conv_kernel(gcp_ref, gc_ref, gcn_ref, hp_ref, h_ref, hn_ref, gb_ref, z_ref, w_ref,
                       o_ref, ext_ref, *, ts):
    i = pl.program_id(1)
    mul = lambda a, b: a.astype(F32) * b.astype(F32)
    _fill_ext(ext_ref, mul(gcp_ref[...], hp_ref[...]), mul(gc_ref[...], h_ref[...]),
              mul(gcn_ref[...], hn_ref[...]), i, pl.num_programs(1), ts)

    def emit(r0, sl, acc):
        rs = pl.ds(r0, acc.shape[0])
        y = gb_ref[rs, sl].astype(F32) * acc * _silu(z_ref[rs, sl].astype(F32))
        o_ref[rs, sl] = y.astype(o_ref.dtype)

    _dwconv(ext_ref, w_ref, SC_K, ts, SC_W, emit)


def _short_conv(uo, w, batch, seq, ts, col0):
    n = batch * seq
    nblk = seq // ts
    gc_specs = _halo_specs(ts, SC_W, lambda: col0 + 1, nblk, batch)
    h_specs = _halo_specs(ts, SC_W, lambda: col0 + 2, nblk, batch)
    return pl.pallas_call(
        functools.partial(_short_conv_kernel, ts=ts),
        grid=(batch, nblk),
        in_specs=gc_specs + h_specs + [
            pl.BlockSpec((ts, SC_W), lambda b, i: (b * nblk + i, col0)),
            pl.BlockSpec((ts, SC_W), lambda b, i: (b * nblk + i, col0 + 3)),
            pl.BlockSpec(w.shape, lambda b, i: (0, 0))],
        out_specs=pl.BlockSpec((ts, SC_W), lambda b, i: (b * nblk + i, 0)),
        out_shape=jax.ShapeDtypeStruct((n, SC_W), BF16),
        scratch_shapes=[pltpu.VMEM((ts + 2 * HALO, SC_W), F32)],
        compiler_params=_cparams(("parallel", "parallel")),
        name="short_conv",
    )(uo, uo, uo, uo, uo, uo, uo, uo, w)


def _out_odd_kernel(x_ref, y_ref, xs_ref, z_ref, dskip_ref, g_ref, od_ref, w1_ref, w2_ref, *rest):
    o_ref = rest[-1]
    gw = SSD_HPG * SSD_HEAD_DIM
    y = y_ref[0].astype(F32) + y_ref[1].astype(F32) + dskip_ref[...] * xs_ref[...].astype(F32)
    y = y * _silu(z_ref[...].astype(F32))
    oc = jnp.concatenate([_rms(y[:, g * gw:(g + 1) * gw], g_ref[:, g * gw:(g + 1) * gw]).astype(BF16)
                          for g in range(SSD_GROUPS)], axis=1)
    acc = jnp.dot(oc, w1_ref[...], preferred_element_type=F32)
    acc = acc + jnp.dot(od_ref[...], w2_ref[...], preferred_element_type=F32)
    out = x_ref[...] + acc
    if len(rest) == 2:
        out = _rms(out, rest[0][...])
    o_ref[...] = out


def _out_odd(x2d, y2, xbc, uo, dskip, ssd_g, od, w1, w2, final_g, tm):
    n = x2d.shape[0]
    extra = [] if final_g is None else [final_g.reshape(1, -1).astype(F32)]
    return pl.pallas_call(
        _out_odd_kernel,
        grid=(n // tm,),
        in_specs=[
            pl.BlockSpec((tm, D_MODEL), lambda i: (i, 0)),
            pl.BlockSpec((2, tm, SSD_W), lambda i: (0, i, 0)),
            pl.BlockSpec((tm, SSD_W), lambda i: (i, 0)),
            pl.BlockSpec((tm, SSD_W), lambda i: (i, 0)),
            pl.BlockSpec((1, SSD_W), lambda i: (0, 0)),
            pl.BlockSpec((1, SSD_W), lambda i: (0, 0)),
            pl.BlockSpec((tm, SC_W), lambda i: (i, 0)),
            pl.BlockSpec(w1.shape, lambda i: (0, 0)),
            pl.BlockSpec(w2.shape, lambda i: (0, 0)),
        ] + [pl.BlockSpec((1, D_MODEL), lambda i: (0, 0)) for _ in extra],
        out_specs=pl.BlockSpec((tm, D_MODEL), lambda i: (i, 0)),
        out_shape=jax.ShapeDtypeStruct((n, D_MODEL), F32),
        compiler_params=_cparams(("parallel",)),
        name="out_proj_odd",
    )(x2d, y2, xbc, uo, dskip, ssd_g, od, w1, w2, *extra)


def _final_norm_kernel(x_ref, g_ref, o_ref):
    o_ref[...] = _rms(x_ref[...], g_ref[...])


def _final_norm(x2d, g, tm):
    n = x2d.shape[0]
    return pl.pallas_call(
        _final_norm_kernel,
        grid=(n // tm,),
        in_specs=[pl.BlockSpec((tm, D_MODEL), lambda i: (i, 0)), pl.BlockSpec((1, D_MODEL), lambda i: (0, 0))],
        out_specs=pl.BlockSpec((tm, D_MODEL), lambda i: (i, 0)),
        out_shape=jax.ShapeDtypeStruct((n, D_MODEL), F32),
        compiler_params=_cparams(("parallel",)),
        name="final_norm",
    )(x2d, g)


def _tile(total, want):
    return want if total % want == 0 else total


def _even_layer(x2d, batch, seq, norm_g, w_in, w_uq, q_g, w_ukv, kv_g, conv_w, conv_b, ln_g, ln_b, w_out,
                tables):
    n = batch * seq
    low = Q_LORA + KV_LORA + QK_ROPE
    row = lambda v: v.reshape(1, -1).astype(F32)

    wa = jnp.pad(w_in[:, :low], ((0, 0), (0, 6 * LANES - low))).astype(BF16)
    w_rest = w_in[:, low:].astype(BF16)
    wq = w_uq.reshape(Q_LORA, MLA_HEADS, QK_NOPE + QK_ROPE)
    wqt = jnp.pad(wq, ((0, 0), (0, 0), (0, QK_PAD - QK_NOPE - QK_ROPE)))
    wqt = wqt.reshape(Q_LORA, MLA_HEADS * QK_PAD).T.astype(BF16)
    wkv = w_ukv.reshape(KV_LORA, MLA_HEADS, QK_NOPE + V_DIM)
    wk = jnp.pad(wkv[:, :, :QK_NOPE], ((0, 0), (0, 0), (0, QK_PAD - QK_NOPE)))
    wk = wk.reshape(KV_LORA, MLA_HEADS * QK_PAD).astype(BF16)
    wvt = jnp.pad(wkv[:, :, QK_NOPE:], ((0, 0), (0, 0), (0, VT_ROWS - V_DIM)))
    wvt = wvt.reshape(KV_LORA, MLA_HEADS * VT_ROWS).T.astype(BF16)
    cols = jnp.arange(MLA_HEADS * QK_PAD)
    place = ((cols[None, :] % QK_PAD) == (jnp.arange(LANES)[:, None] + QK_NOPE))
    place = (place & (jnp.arange(LANES)[:, None] < QK_ROPE)).astype(BF16)
    ones_col = ((jnp.arange(MLA_HEADS * VT_ROWS) % VT_ROWS) == V_DIM).astype(F32)[:, None]

    tm = _tile(seq, TOKEN_TILE)
    u2 = _rms_matmul(x2d, row(norm_g), w_rest, _tile(n, PROJ_ROWS), _tile(w_rest.shape[1], PROJ_COLS),
                     BF16)
    qt, k, vt = _mla_prep(x2d, row(norm_g), wa, row(q_g), row(kv_g), wqt, wk, place, wvt, ones_col,
                          *tables, batch, seq, tm)
    ot = _attention(qt, k, vt, batch, seq, _tile(seq, QUERY_TILE), tm)
    oa = _conformer(u2, conv_w.astype(F32), row(conv_b), row(ln_g), row(ln_b), batch, seq,
                    _tile(seq, CONV_TILE))
    w_out = w_out.astype(BF16)
    return _out_even(x2d, ot, u2, oa, w_out[:MLA_W], w_out[MLA_W:], batch, seq, _tile(seq, OUT_ROWS))


def _odd_layer(x2d, batch, seq, norm_g, w_in, conv_c_w, conv_c_b, dt_bias_f, dt_bias_b, a_log_f, a_log_b,
               d_skip, ssd_g, conv_d_w, w_out, final_g):
    n = batch * seq
    row = lambda v: v.reshape(1, -1).astype(F32)
    o_dt = SSD_W + XBC_W
    w_main = jnp.concatenate([w_in[:, :o_dt], w_in[:, o_dt + 2 * SSD_HEADS:]], axis=1).astype(BF16)
    w_dt = w_in[:, o_dt:o_dt + 2 * SSD_HEADS].reshape(D_MODEL, 2 * SSD_GROUPS, SSD_HPG)
    wdt_t = jnp.pad(w_dt, ((0, 0), (0, 0), (0, 8 - SSD_HPG))).reshape(D_MODEL, 64).T.astype(BF16)

    def per_head(vf, vb):
        v = jnp.concatenate([vf, vb]).reshape(2, SSD_GROUPS, SSD_HPG)
        v = jnp.pad(v, ((0, 0), (0, 0), (0, 8 - SSD_HPG)))
        return jnp.broadcast_to(v[..., None], (2, SSD_GROUPS, 8, SSD_CHUNK)).astype(F32)

    tm = _tile(seq, TOKEN_TILE)
    uo, dt_t = _rms_matmul(x2d, row(norm_g), w_main, _tile(n, PROJ_ROWS), _tile(w_main.shape[1], PROJ_COLS),
                           BF16, wt=wdt_t)
    dt_t = dt_t.reshape(2, SSD_GROUPS, 8, n)
    xbc = _xbc_conv(uo, conv_c_w.astype(F32), row(conv_c_b), batch, seq, _tile(seq, CONV_TILE), 512,
                    SSD_W // 512)
    y2 = _ssd(xbc, dt_t, per_head(dt_bias_f, dt_bias_b), per_head(a_log_f, a_log_b), batch, seq)
    dskip = row(jnp.repeat(d_skip, SSD_HEAD_DIM))
    od = _short_conv(uo, conv_d_w.astype(F32), batch, seq, _tile(seq, CONV_TILE), (SSD_W + XBC_W) // SC_W)
    w_out = w_out.astype(BF16)
    return _out_odd(x2d, y2, xbc, uo, dskip, row(ssd_g), od, w_out[:SSD_W], w_out[SSD_W:], final_g, tm)


def _rope_tables(seq):
    half = QK_ROPE // 2
    inv = ROPE_THETA ** (-jnp.arange(half, dtype=F32) / half)
    ang = jnp.arange(seq, dtype=F32)[:, None] * inv[None, :]
    cos, sin = jnp.cos(ang), jnp.sin(ang)
    pad = ((0, 0), (0, LANES - QK_ROPE))
    cos2 = jnp.pad(jnp.concatenate([cos, cos], axis=1), pad)
    sin2 = jnp.pad(jnp.concatenate([-sin, sin], axis=1), pad)
    return cos2, sin2, cos.T, sin.T


def kernel(x, norm_e, w_in_e, w_uq, q_norm, w_ukv, kv_norm, conv_a_w, conv_a_b, ln_a_g, ln_a_b, w_out_e, norm_o, w_in_o, conv_c_w, conv_c_b, dt_bias_f, dt_bias_b, a_log_f, a_log_b, d_skip, ssd_norm, conv_d_w, w_out_o, final_norm):
    batch, seq, _ = x.shape
    depth = norm_e.shape[0] + norm_o.shape[0]
    tables = _rope_tables(seq)
    x2d = x.reshape(batch * seq, D_MODEL)
    for i in range(depth):
        j = i // 2
        if i % 2 == 0:
            x2d = _even_layer(x2d, batch, seq, norm_e[j], w_in_e[j], w_uq[j], q_norm[j], w_ukv[j], kv_norm[j],
                              conv_a_w[j], conv_a_b[j], ln_a_g[j], ln_a_b[j], w_out_e[j], tables)
        else:
            last = i == depth - 1
            x2d = _odd_layer(x2d, batch, seq, norm_o[j], w_in_o[j], conv_c_w[j], conv_c_b[j], dt_bias_f[j],
                             dt_bias_b[j], a_log_f[j], a_log_b[j], d_skip[j], ssd_norm[j], conv_d_w[j],
                             w_out_o[j], final_norm if last else None)
    if depth % 2 == 1:
        x2d = _final_norm(x2d, final_norm.reshape(1, -1), _tile(batch * seq, TOKEN_TILE))
    return x2d.reshape(batch, seq, D_MODEL)
```

```python
import functools
import math

import jax
import jax.numpy as jnp
from jax import lax
from jax.experimental import pallas as pl
from jax.experimental.pallas import tpu as pltpu

F32 = jnp.float32
BF16 = jnp.bfloat16
EPS = 1e-6

D_MODEL = 1024
MLA_HEADS = 16
QK_NOPE = 64
QK_ROPE = 32
V_DIM = 64
Q_LORA = 384
KV_LORA = 256
MLA_W = MLA_HEADS * V_DIM
ROPE_THETA = 10000.0
CONV_W = 1024
CONV_K = 31
SSD_HEAD_DIM = 64
SSD_HEADS = 24
SSD_W = SSD_HEADS * SSD_HEAD_DIM
SSD_GROUPS = 4
SSD_HPG = SSD_HEADS // SSD_GROUPS
SSD_STATE = 128
SSD_CONV_K = 5
SSD_CHUNK = 128
XBC_W = SSD_W + 2 * SSD_GROUPS * SSD_STATE
SC_W = 512
SC_K = 3

LANES = 128
HALO = 16
QK_PAD = 128
VT_ROWS = 80
ATTN_SAFE_MAX = 1e30
ATTN_QK_GROUP = 2
VMEM_LIMIT = 56 * 1024 * 1024

TOKEN_TILE = 512
PREP_TILE = 1024
CONV_TILE = 1024
NARROW_CONV_TILE = 2048
QUERY_TILE = 2048
PROJ_ROWS = 1024
PROJ_COLS = 2048
OUT_ROWS = 1024
SSD_CHUNKS_PER_STEP = 8


def _cparams(sem):
    return pltpu.CompilerParams(dimension_semantics=sem, vmem_limit_bytes=VMEM_LIMIT)


def _silu(v):
    return v * (1.0 / (1.0 + jnp.exp(-v)))


def _softplus(v):
    return jnp.maximum(v, 0.0) + jnp.log(1.0 + jnp.exp(-jnp.abs(v)))


def _rms(v, g):
    ms = jnp.mean(v * v, axis=-1, keepdims=True)
    return v * lax.rsqrt(ms + EPS) * g


def _rms_matmul_kernel(x_ref, g_ref, w_ref, *rest):
    o_ref, xn_ref = rest[-3 if len(rest) == 4 else -2], rest[-1]

    @pl.when(pl.program_id(1) == 0)
    def _():
        xn_ref[...] = _rms(x_ref[...], g_ref[...]).astype(xn_ref.dtype)
        if len(rest) == 4:
            rest[2][...] = lax.dot_general(rest[0][...], xn_ref[...], (((1,), (1,)), ((), ())),
                                           preferred_element_type=F32)

    o_ref[...] = jnp.dot(xn_ref[...], w_ref[...], preferred_element_type=F32).astype(o_ref.dtype)


def _rms_matmul(x, g, w, tm, tn, out_dtype, wt=None):
    n, d = x.shape
    e = w.shape[1]
    in_specs = [
        pl.BlockSpec((tm, d), lambda i, j: (i, 0)),
        pl.BlockSpec((1, d), lambda i, j: (0, 0)),
        pl.BlockSpec((d, tn), lambda i, j: (0, j)),
    ]
    out_specs = [pl.BlockSpec((tm, tn), lambda i, j: (i, j))]
    out_shape = [jax.ShapeDtypeStruct((n, e), out_dtype)]
    args = [x, g, w]
    if wt is not None:
        in_specs.append(pl.BlockSpec(wt.shape, lambda i, j: (0, 0)))
        out_specs.append(pl.BlockSpec((wt.shape[0], tm), lambda i, j: (0, i)))
        out_shape.append(jax.ShapeDtypeStruct((wt.shape[0], n), F32))
        args.append(wt)
    outs = pl.pallas_call(
        _rms_matmul_kernel,
        grid=(n // tm, e // tn),
        in_specs=in_specs,
        out_specs=out_specs,
        out_shape=out_shape,
        scratch_shapes=[pltpu.VMEM((tm, d), BF16)],
        compiler_params=_cparams(("parallel", "arbitrary")),
        name="rms_matmul",
    )(*args)
    return outs[0] if wt is None else outs


def _mla_prep_kernel(x_ref, g_ref, wa_ref, qg_ref, kvg_ref, wqt_ref, wk_ref, wvt_ref,
                     ones_ref, cos2_ref, sin2_ref, cost_ref, sint_ref,
                     qt_ref, k_ref, vt_ref, *, qscale):
    xn = _rms(x_ref[...], g_ref[...]).astype(BF16)
    a = jnp.dot(xn, wa_ref[...], preferred_element_type=F32)
    qn = _rms(a[:, :Q_LORA], qg_ref[...]).astype(BF16)
    kvn = _rms(a[:, Q_LORA:Q_LORA + KV_LORA], kvg_ref[...]).astype(BF16)

    kr = a[:, Q_LORA + KV_LORA:]
    lane = lax.broadcasted_iota(jnp.int32, kr.shape, 1)
    half = QK_ROPE // 2
    swapped = jnp.where(lane < half, pltpu.roll(kr, LANES - half, 1), pltpu.roll(kr, half, 1))
    kr = kr * cos2_ref[...] + swapped * sin2_ref[...]

    k = jnp.dot(kvn, wk_ref[...], preferred_element_type=F32)
    k = k + jnp.tile(pltpu.roll(kr, QK_NOPE, 1), (1, MLA_HEADS))
    k_ref[...] = k.astype(k_ref.dtype)

    nt = (((1,), (1,)), ((), ()))
    vt = lax.dot_general(wvt_ref[...], kvn, nt, preferred_element_type=F32) + ones_ref[...]
    tk = vt_ref.shape[-1]
    for c in range(vt_ref.shape[0]):
        vt_ref[c] = vt[:, c * tk:(c + 1) * tk].astype(vt_ref.dtype)

    qt = lax.dot_general(wqt_ref[...], qn, nt, preferred_element_type=F32) * qscale
    qt_ref[...] = qt.astype(qt_ref.dtype)
    cos_t = cost_ref[...]
    sin_t = sint_ref[...]
    for h in range(MLA_HEADS):
        r0 = h * QK_PAD + QK_NOPE
        x1 = qt[r0:r0 + half]
        x2 = qt[r0 + half:r0 + 2 * half]
        qt_ref[r0:r0 + half, :] = (x1 * cos_t - x2 * sin_t).astype(qt_ref.dtype)
        qt_ref[r0 + half:r0 + 2 * half, :] = (x2 * cos_t + x1 * sin_t).astype(qt_ref.dtype)


def _mla_prep(x2d, g, wa, qg, kvg, wqt, wk, wvt, ones_col, cos2, sin2, cos_t, sin_t,
              batch, seq, tm, tk):
    nblk = seq // tm
    qscale = (QK_NOPE + QK_ROPE) ** -0.5 * math.log2(math.e)
    full = lambda arr: pl.BlockSpec(arr.shape, lambda b, i: (0,) * arr.ndim)
    hq = MLA_HEADS * QK_PAD
    hv = MLA_HEADS * VT_ROWS
    return pl.pallas_call(
        functools.partial(_mla_prep_kernel, qscale=qscale),
        grid=(batch, nblk),
        in_specs=[
            pl.BlockSpec((tm, D_MODEL), lambda b, i: (b * nblk + i, 0)),
            full(g), full(wa), full(qg), full(kvg), full(wqt), full(wk), full(wvt),
            full(ones_col),
            pl.BlockSpec((tm, LANES), lambda b, i: (i, 0)),
            pl.BlockSpec((tm, LANES), lambda b, i: (i, 0)),
            pl.BlockSpec((QK_ROPE // 2, tm), lambda b, i: (0, i)),
            pl.BlockSpec((QK_ROPE // 2, tm), lambda b, i: (0, i)),
        ],
        out_specs=[
            pl.BlockSpec((None, hq, tm), lambda b, i: (b, 0, i)),
            pl.BlockSpec((None, tm, hq), lambda b, i: (b, i, 0)),
            pl.BlockSpec((None, tm // tk, hv, tk), lambda b, i: (b, i, 0, 0)),
        ],
        out_shape=[
            jax.ShapeDtypeStruct((batch, hq, seq), BF16),
            jax.ShapeDtypeStruct((batch, seq, hq), BF16),
            jax.ShapeDtypeStruct((batch, seq // tk, hv, tk), BF16),
        ],
        compiler_params=_cparams(("parallel", "parallel")),
        name="mla_prep",
    )(x2d, g, wa, qg, kvg, wqt, wk, wvt, ones_col, cos2, sin2, cos_t, sin_t)


def _attn_kernel(qt_ref, qn_ref, k_ref, vt_ref, o_ref, m_ref, acc_ref, s0_ref, c0_ref, *, tk, nk):
    def first_scores(q_ref):
        st = jnp.dot(k_ref[0:tk, :], q_ref[...], preferred_element_type=F32)
        s0_ref[...] = st
        c0_ref[...] = jnp.max(st, axis=0, keepdims=True)

    @pl.when(pl.program_id(2) == 0)
    def _():
        first_scores(qt_ref)

    qt = qt_ref[...]
    m0 = c0_ref[...]
    acc = jnp.dot(vt_ref[0], jnp.exp2(s0_ref[...] - m0).astype(BF16), preferred_element_type=F32)
    c = 1
    while c < nk:
        c_end = min(nk, (c // ATTN_QK_GROUP + 1) * ATTN_QK_GROUP)
        st = jnp.dot(k_ref[c * tk:c_end * tk, :], qt, preferred_element_type=F32)
        for u in range(c_end - c):
            p = jnp.exp2(st[u * tk:(u + 1) * tk] - m0).astype(BF16)
            acc = acc + jnp.dot(vt_ref[c + u], p, preferred_element_type=F32)
        c = c_end
    acc_ref[...] = acc
    first_scores(qn_ref)

    used = acc[:V_DIM + 8]
    unsafe = jnp.max(jnp.where(jnp.abs(used) < ATTN_SAFE_MAX, 0.0, 1.0)) > 0.0

    @pl.when(unsafe)
    def _():
        m_ref[...] = jnp.full(m_ref.shape, -jnp.inf, F32)
        acc_ref[...] = jnp.zeros(acc_ref.shape, F32)

        def body(c, carry):
            k0 = pl.multiple_of(c * tk, tk)
            st = jnp.dot(k_ref[pl.ds(k0, tk), :], qt, preferred_element_type=F32)
            m_old = m_ref[...]
            m_new = jnp.maximum(m_old, jnp.max(st, axis=0, keepdims=True))
            p = jnp.exp2(st - m_new).astype(BF16)
            pv = jnp.dot(vt_ref[c], p, preferred_element_type=F32)
            acc_ref[...] = acc_ref[...] * jnp.exp2(m_old - m_new) + pv
            m_ref[...] = m_new
            return carry

        lax.fori_loop(0, nk, body, 0)

    acc = acc_ref[...]
    o_ref[...] = (acc[:V_DIM] * (1.0 / acc[V_DIM:V_DIM + 1])).astype(o_ref.dtype)


def _attention(qt, k, vt, batch, seq, tq, tk):
    nk = seq // tk
    nq = seq // tq
    return pl.pallas_call(
        functools.partial(_attn_kernel, tk=tk, nk=nk),
        grid=(batch, MLA_HEADS, nq),
        in_specs=[
            pl.BlockSpec((None, QK_PAD, tq), lambda b, h, i: (b, h, i)),
            pl.BlockSpec((None, QK_PAD, tq), lambda b, h, i: (b, h, jnp.minimum(i + 1, nq - 1))),
            pl.BlockSpec((None, seq, QK_PAD), lambda b, h, i: (b, 0, h)),
            pl.BlockSpec((None, nk, VT_ROWS, tk), lambda b, h, i: (b, 0, h, 0)),
        ],
        out_specs=pl.BlockSpec((None, V_DIM, tq), lambda b, h, i: (b, h, i)),
        out_shape=jax.ShapeDtypeStruct((batch, MLA_W, seq), BF16),
        scratch_shapes=[pltpu.VMEM((1, tq), F32), pltpu.VMEM((VT_ROWS, tq), F32),
                        pltpu.VMEM((tk, tq), F32), pltpu.VMEM((1, tq), F32)],
        compiler_params=_cparams(("parallel", "parallel", "arbitrary")),
        name="mla_attention",
    )(qt, qt, k, vt)


def _fill_ext(ext_ref, prev, cur, nxt, i, n_i, ts):
    ext_ref[0:HALO, :] = jnp.where(i > 0, prev, 0.0)
    ext_ref[HALO:HALO + ts, :] = cur
    ext_ref[HALO + ts:2 * HALO + ts, :] = jnp.where(i < n_i - 1, nxt, 0.0)


def _dwconv(ext_ref, w_ref, k_taps, ts, width, emit, rows=128):
    def body(rc, carry):
        r0 = pl.multiple_of(rc * rows, rows)
        for cb in range(width // LANES):
            sl = slice(cb * LANES, (cb + 1) * LANES)
            emit(r0, sl, _dwconv_block(ext_ref, w_ref, k_taps, r0, rows, sl))
        return carry

    lax.fori_loop(0, ts // rows, body, 0)


def _dwconv_block(ext_ref, w_ref, k_taps, r0, rows, sl):
    sub = 8
    off = HALO - k_taps // 2
    lo = (off // sub) * sub
    win_rows = ((off - lo + k_taps - 1) // sub + 1) * sub + rows
    win = ext_ref[pl.ds(r0 + lo, win_rows), sl]
    acc = jnp.zeros((rows, LANES), F32)
    for phase in range(sub):
        taps = [k for k in range(k_taps) if (off - lo + k) % sub == phase]
        if not taps:
            continue
        shifted = win if phase == 0 else pltpu.roll(win, win_rows - phase, 0)
        for k in taps:
            q = (off - lo + k) // sub * sub
            acc = acc + w_ref[k:k + 1, sl] * shifted[q:q + rows]
    return acc


def _halo_specs(ts, width, col, nblk, batch):
    per = ts // HALO
    last = batch * nblk * per - 1

    def prev_map(b, i, *_):
        return (jnp.maximum((b * nblk + i) * per - 1, 0), col(*_))

    def cur_map(b, i, *_):
        return (b * nblk + i, col(*_))

    def next_map(b, i, *_):
        return (jnp.minimum((b * nblk + i + 1) * per, last), col(*_))

    return [pl.BlockSpec((HALO, width), prev_map), pl.BlockSpec((ts, width), cur_map),
            pl.BlockSpec((HALO, width), next_map)]


def _conformer_kernel(ap_ref, a_ref, an_ref, gp_ref, g_ref, gn_ref, z_ref, w_ref, b_ref, lng_ref,
                      lnb_ref, o_ref, ext_ref, conv_ref, *, ts):
    i = pl.program_id(1)
    n_i = pl.num_programs(1)

    def glu(a, g):
        return a.astype(F32) * (1.0 / (1.0 + jnp.exp(-g.astype(F32))))

    _fill_ext(ext_ref, glu(ap_ref[...], gp_ref[...]), glu(a_ref[...], g_ref[...]),
              glu(an_ref[...], gn_ref[...]), i, n_i, ts)

    def emit(r0, sl, acc):
        conv_ref[pl.ds(r0, acc.shape[0]), sl] = acc + b_ref[:, sl]

    _dwconv(ext_ref, w_ref, CONV_K, ts, CONV_W, emit)

    rows = 128

    def ln_body(rc, carry):
        r0 = pl.multiple_of(rc * rows, rows)
        v = conv_ref[pl.ds(r0, rows), :]
        vc = v - jnp.mean(v, axis=-1, keepdims=True)
        var = jnp.mean(vc * vc, axis=-1, keepdims=True)
        y = vc * lax.rsqrt(var + EPS) * lng_ref[...] + lnb_ref[...]
        y = _silu(y) * _silu(z_ref[pl.ds(r0, rows), :].astype(F32))
        o_ref[pl.ds(r0, rows), :] = y.astype(o_ref.dtype)
        return carry

    lax.fori_loop(0, ts // rows, ln_body, 0)


def _conformer(u2, w, b, lng, lnb, batch, seq, ts):
    n = batch * seq
    nblk = seq // ts
    a_specs = _halo_specs(ts, CONV_W, lambda: 1, nblk, batch)
    g_specs = _halo_specs(ts, CONV_W, lambda: 2, nblk, batch)
    small = lambda arr: pl.BlockSpec(arr.shape, lambda b_, i: (0, 0))
    return pl.pallas_call(
        functools.partial(_conformer_kernel, ts=ts),
        grid=(batch, nblk),
        in_specs=a_specs + g_specs + [
            pl.BlockSpec((ts, CONV_W), lambda b_, i: (b_ * nblk + i, 3)),
            small(w), small(b), small(lng), small(lnb)],
        out_specs=pl.BlockSpec((ts, CONV_W), lambda b_, i: (b_ * nblk + i, 0)),
        out_shape=jax.ShapeDtypeStruct((n, CONV_W), BF16),
        scratch_shapes=[pltpu.VMEM((ts + 2 * HALO, CONV_W), F32), pltpu.VMEM((ts, CONV_W), F32)],
        compiler_params=_cparams(("parallel", "parallel")),
        name="conformer_conv",
    )(u2, u2, u2, u2, u2, u2, u2, w, b, lng, lnb)


def _out_even_kernel(x_ref, ot_ref, z_ref, oa_ref, w1_ref, w2_ref, o_ref):
    ob = ot_ref[...].astype(F32).T * _silu(z_ref[...].astype(F32))
    acc = jnp.dot(ob.astype(BF16), w1_ref[...], preferred_element_type=F32)
    acc = acc + jnp.dot(oa_ref[...], w2_ref[...], preferred_element_type=F32)
    o_ref[...] = x_ref[...] + acc


def _out_even(x2d, ot, u2, oa, w1, w2, batch, seq, tm):
    n = batch * seq
    nblk = seq // tm
    return pl.pallas_call(
        _out_even_kernel,
        grid=(batch, nblk),
        in_specs=[
            pl.BlockSpec((tm, D_MODEL), lambda b, i: (b * nblk + i, 0)),
            pl.BlockSpec((None, MLA_W, tm), lambda b, i: (b, 0, i)),
            pl.BlockSpec((tm, MLA_W), lambda b, i: (b * nblk + i, 0)),
            pl.BlockSpec((tm, CONV_W), lambda b, i: (b * nblk + i, 0)),
            pl.BlockSpec(w1.shape, lambda b, i: (0, 0)),
            pl.BlockSpec(w2.shape, lambda b, i: (0, 0)),
        ],
        out_specs=pl.BlockSpec((tm, D_MODEL), lambda b, i: (b * nblk + i, 0)),
        out_shape=jax.ShapeDtypeStruct((n, D_MODEL), F32),
        compiler_params=_cparams(("parallel", "parallel")),
        name="out_proj_even",
    )(x2d, ot, u2, oa, w1, w2)


def _xbc_conv_kernel(p_ref, c_ref, n_ref, w_ref, b_ref, o_ref, ext_ref, *, ts, width):
    i = pl.program_id(1)
    _fill_ext(ext_ref, p_ref[...].astype(F32), c_ref[...].astype(F32), n_ref[...].astype(F32),
              i, pl.num_programs(1), ts)

    def emit(r0, sl, acc):
        o_ref[pl.ds(r0, acc.shape[0]), sl] = _silu(acc + b_ref[:, sl]).astype(o_ref.dtype)

    _dwconv(ext_ref, w_ref, SSD_CONV_K, ts, width, emit)


def _xbc_conv(uo, w, b, batch, seq, ts, width, col0):
    n = batch * seq
    nblk = seq // ts
    ncol = XBC_W // width
    specs = _halo_specs(ts, width, lambda j: col0 + j, nblk, batch)
    return pl.pallas_call(
        functools.partial(_xbc_conv_kernel, ts=ts, width=width),
        grid=(batch, nblk, ncol),
        in_specs=specs + [pl.BlockSpec((SSD_CONV_K, width), lambda b_, i, j: (0, j)),
                          pl.BlockSpec((1, width), lambda b_, i, j: (0, j))],
        out_specs=pl.BlockSpec((ts, width), lambda b_, i, j: (b_ * nblk + i, j)),
        out_shape=jax.ShapeDtypeStruct((n, XBC_W), BF16),
        scratch_shapes=[pltpu.VMEM((ts + 2 * HALO, width), F32)],
        compiler_params=_cparams(("parallel", "parallel", "parallel")),
        name="xbc_conv",
    )(uo, uo, uo, w, b)


def _ssd_kernel(xbc_ref, dt_ref, bias_ref, alog_ref, y_ref, state_ref, lhs_ref, btw_ref, *, cps):
    d = pl.program_id(1)
    c = pl.program_id(2)
    L = SSD_CHUNK
    G = SSD_GROUPS
    pairs = SSD_HPG // 2
    gw = SSD_HPG * SSD_HEAD_DIM

    @pl.when(c == 0)
    def _():
        state_ref[...] = jnp.zeros(state_ref.shape, F32)

    row = lax.broadcasted_iota(jnp.int32, (L, L), 0)
    col = lax.broadcasted_iota(jnp.int32, (L, L), 1)
    lane_lo = col < SSD_HEAD_DIM
    ahead = (row - col) * (1 - 2 * d)
    mask = ahead >= 0
    tri = jnp.where(ahead <= 0, 1.0, 0.0)
    stack_rows = lambda ref: jnp.concatenate([ref[g] for g in range(G)], axis=0)
    bias = stack_rows(bias_ref)
    neg_a = -math.log2(math.e) * jnp.exp(stack_rows(alog_ref))
    dt_all = stack_rows(dt_ref)

    def operands(k, pos):
        rows = pl.ds(pl.multiple_of(pos * L, L), L)
        dt_raw = dt_all[:, 0:L]
        for q in range(1, cps):
            dt_raw = jnp.where(pos == q, dt_all[:, q * L:(q + 1) * L], dt_raw)
        dtv = _softplus(dt_raw + bias)
        da = dtv * neg_a
        cs_t = jnp.dot(da, tri, preferred_element_type=F32, precision=lax.Precision.HIGHEST)
        tot = jnp.sum(da, axis=1, keepdims=True)
        etot = jnp.exp2(jnp.broadcast_to(tot, cs_t.shape))
        w_t = jnp.exp2(tot - cs_t) * dtv
        srow_t = cs_t - jnp.log2(dtv)
        cs_cols = jnp.concatenate([cs_t, jnp.zeros((L - 8 * G, L), F32)], axis=0).T
        for g in range(G):
            bm = xbc_ref[rows, SSD_W + g * SSD_STATE:SSD_W + (g + 1) * SSD_STATE]
            cm = xbc_ref[rows, SSD_W + (G + g) * SSD_STATE:SSD_W + (G + g + 1) * SSD_STATE]
            cb = lax.dot_general(cm, bm, (((1,), (1,)), ((), ())), preferred_element_type=F32)
            bt = bm.astype(F32).T
            cmf = cm.astype(F32)
            for rr in range(SSD_HPG):
                r = 8 * g + rr
                i = g * SSD_HPG + rr
                cs_col = jnp.broadcast_to(cs_cols[:, r:r + 1], (L, L))
                m = jnp.where(mask, jnp.exp2(cs_col - srow_t[r:r + 1, :]), 0.0) * cb
                lhs_ref[k, i, :, 0:L] = m.astype(BF16)
                lhs_ref[k, i, :, L:2 * L] = (cmf * jnp.exp2(cs_col)).astype(BF16)
                btw_ref[k, i] = (bt * w_t[r:r + 1, :]).astype(BF16)
        return etot

    def outputs(k, pos, etot):
        rows = pl.ds(pl.multiple_of(pos * L, L), L)
        for g in range(G):
            for j in range(pairs):
                lanes = slice(g * gw + j * LANES, g * gw + (j + 1) * LANES)
                xs_pair = xbc_ref[rows, lanes]
                st_pair = state_ref[g * pairs + j]
                rhs = jnp.concatenate([xs_pair, st_pair.astype(BF16)], axis=0)
                i0 = g * SSD_HPG + 2 * j
                ys = [jnp.dot(lhs_ref[k, i0 + u], rhs, preferred_element_type=F32) for u in range(2)]
                news = [jnp.dot(btw_ref[k, i0 + u], xs_pair, preferred_element_type=F32) for u in range(2)]
                y_ref[rows, lanes] = jnp.where(lane_lo, ys[0], ys[1]).astype(y_ref.dtype)
                r0 = 8 * g + 2 * j
                decay = jnp.where(lane_lo[:1], etot[r0:r0 + 1, :], etot[r0 + 1:r0 + 2, :])
                state_ref[g * pairs + j] = st_pair * decay + jnp.where(lane_lo, news[0], news[1])

    positions = [jnp.where(d == 0, k, cps - 1 - k) for k in range(cps)]
    decays = [operands(k, positions[k]) for k in range(cps)]
    for k in range(cps):
        outputs(k, positions[k], decays[k])


def _ssd(xbc, dt_t, bias, alog, batch, seq):
    n = batch * seq
    nc = seq // SSD_CHUNK
    cps = SSD_CHUNKS_PER_STEP if nc % SSD_CHUNKS_PER_STEP == 0 else 1
    nsteps = nc // cps
    rows = cps * SSD_CHUNK

    def tpos(b, d, c):
        return b * nsteps + jnp.where(d == 0, c, nsteps - 1 - c)

    head_rows = (None, SSD_GROUPS, 8, SSD_CHUNK)
    return pl.pallas_call(
        functools.partial(_ssd_kernel, cps=cps),
        grid=(batch, 2, nsteps),
        in_specs=[
            pl.BlockSpec((rows, XBC_W), lambda b, d, c: (tpos(b, d, c), 0)),
            pl.BlockSpec((None, SSD_GROUPS, 8, rows), lambda b, d, c: (d, 0, 0, tpos(b, d, c))),
            pl.BlockSpec(head_rows, lambda b, d, c: (d, 0, 0, 0)),
            pl.BlockSpec(head_rows, lambda b, d, c: (d, 0, 0, 0)),
        ],
        out_specs=pl.BlockSpec((None, rows, SSD_W), lambda b, d, c: (d, tpos(b, d, c), 0)),
        out_shape=jax.ShapeDtypeStruct((2, n, SSD_W), BF16),
        scratch_shapes=[pltpu.VMEM((SSD_GROUPS * SSD_HPG // 2, SSD_STATE, LANES), F32),
                        pltpu.VMEM((cps, SSD_HEADS, SSD_CHUNK, 2 * SSD_CHUNK), BF16),
                        pltpu.VMEM((cps, SSD_HEADS, SSD_STATE, SSD_CHUNK), BF16)],
        compiler_params=_cparams(("parallel", "arbitrary", "arbitrary")),
        name="ssd_scan",
    )(xbc, dt_t, bias, alog)


def _short_conv_kernel(gcp_ref, gc_ref, gcn_ref, hp_ref, h_ref, hn_ref, gb_ref, z_ref, w_ref,
                       o_ref, ext_ref, *, ts):
    i = pl.program_id(1)
    mul = lambda a, b: a.astype(F32) * b.astype(F32)
    _fill_ext(ext_ref, mul(gcp_ref[...], hp_ref[...]), mul(gc_ref[...], h_ref[...]),
              mul(gcn_ref[...], hn_ref[...]), i, pl.num_programs(1), ts)

    def emit(r0, sl, acc):
        rs = pl.ds(r0, acc.shape[0])
        y = gb_ref[rs, sl].astype(F32) * acc * _silu(z_ref[rs, sl].astype(F32))
        o_ref[rs, sl] = y.astype(o_ref.dtype)

    _dwconv(ext_ref, w_ref, SC_K, ts, SC_W, emit)


def _short_conv(uo, w, batch, seq, ts, col0):
    n = batch * seq
    nblk = seq // ts
    gc_specs = _halo_specs(ts, SC_W, lambda: col0 + 1, nblk, batch)
    h_specs = _halo_specs(ts, SC_W, lambda: col0 + 2, nblk, batch)
    return pl.pallas_call(
        functools.partial(_short_conv_kernel, ts=ts),
        grid=(batch, nblk),
        in_specs=gc_specs + h_specs + [
            pl.BlockSpec((ts, SC_W), lambda b, i: (b * nblk + i, col0)),
            pl.BlockSpec((ts, SC_W), lambda b, i: (b * nblk + i, col0 + 3)),
            pl.BlockSpec(w.shape, lambda b, i: (0, 0))],
        out_specs=pl.BlockSpec((ts, SC_W), lambda b, i: (b * nblk + i, 0)),
        out_shape=jax.ShapeDtypeStruct((n, SC_W), BF16),
        scratch_shapes=[pltpu.VMEM((ts + 2 * HALO, SC_W), F32)],
        compiler_params=_cparams(("parallel", "parallel")),
        name="short_conv",
    )(uo, uo, uo, uo, uo, uo, uo, uo, w)


def _out_odd_kernel(x_ref, y_ref, xs_ref, z_ref, dskip_ref, g_ref, od_ref, w1_ref, w2_ref, *rest):
    o_ref = rest[-1]
    gw = SSD_HPG * SSD_HEAD_DIM
    y = y_ref[0].astype(F32) + y_ref[1].astype(F32) + dskip_ref[...] * xs_ref[...].astype(F32)
    y = y * _silu(z_ref[...].astype(F32))
    oc = jnp.concatenate([_rms(y[:, g * gw:(g + 1) * gw], g_ref[:, g * gw:(g + 1) * gw]).astype(BF16)
                          for g in range(SSD_GROUPS)], axis=1)
    acc = jnp.dot(oc, w1_ref[...], preferred_element_type=F32)
    acc = acc + jnp.dot(od_ref[...], w2_ref[...], preferred_element_type=F32)
    out = x_ref[...] + acc
    if len(rest) == 2:
        out = _rms(out, rest[0][...])
    o_ref[...] = out


def _out_odd(x2d, y2, xbc, uo, dskip, ssd_g, od, w1, w2, final_g, tm):
    n = x2d.shape[0]
    extra = [] if final_g is None else [final_g.reshape(1, -1).astype(F32)]
    return pl.pallas_call(
        _out_odd_kernel,
        grid=(n // tm,),
        in_specs=[
            pl.BlockSpec((tm, D_MODEL), lambda i: (i, 0)),
            pl.BlockSpec((2, tm, SSD_W), lambda i: (0, i, 0)),
            pl.BlockSpec((tm, SSD_W), lambda i: (i, 0)),
            pl.BlockSpec((tm, SSD_W), lambda i: (i, 0)),
            pl.BlockSpec((1, SSD_W), lambda i: (0, 0)),
            pl.BlockSpec((1, SSD_W), lambda i: (0, 0)),
            pl.BlockSpec((tm, SC_W), lambda i: (i, 0)),
            pl.BlockSpec(w1.shape, lambda i: (0, 0)),
            pl.BlockSpec(w2.shape, lambda i: (0, 0)),
        ] + [pl.BlockSpec((1, D_MODEL), lambda i: (0, 0)) for _ in extra],
        out_specs=pl.BlockSpec((tm, D_MODEL), lambda i: (i, 0)),
        out_shape=jax.ShapeDtypeStruct((n, D_MODEL), F32),
        compiler_params=_cparams(("parallel",)),
        name="out_proj_odd",
    )(x2d, y2, xbc, uo, dskip, ssd_g, od, w1, w2, *extra)


def _final_norm_kernel(x_ref, g_ref, o_ref):
    o_ref[...] = _rms(x_ref[...], g_ref[...])


def _final_norm(x2d, g, tm):
    n = x2d.shape[0]
    return pl.pallas_call(
        _final_norm_kernel,
        grid=(n // tm,),
        in_specs=[pl.BlockSpec((tm, D_MODEL), lambda i: (i, 0)), pl.BlockSpec((1, D_MODEL), lambda i: (0, 0))],
        out_specs=pl.BlockSpec((tm, D_MODEL), lambda i: (i, 0)),
        out_shape=jax.ShapeDtypeStruct((n, D_MODEL), F32),
        compiler_params=_cparams(("parallel",)),
        name="final_norm",
    )(x2d, g)


def _tile(total, want):
    return want if total % want == 0 else total


def _even_layer(x2d, batch, seq, norm_g, w_in, w_uq, q_g, w_ukv, kv_g, conv_w, conv_b, ln_g, ln_b, w_out,
                tables):
    n = batch * seq
    low = Q_LORA + KV_LORA + QK_ROPE
    row = lambda v: v.reshape(1, -1).astype(F32)

    wa = jnp.pad(w_in[:, :low], ((0, 0), (0, 6 * LANES - low))).astype(BF16)
    w_rest = w_in[:, low:].astype(BF16)
    wq = w_uq.reshape(Q_LORA, MLA_HEADS, QK_NOPE + QK_ROPE)
    wqt = jnp.pad(wq, ((0, 0), (0, 0), (0, QK_PAD - QK_NOPE - QK_ROPE)))
    wqt = wqt.reshape(Q_LORA, MLA_HEADS * QK_PAD).T.astype(BF16)
    wkv = w_ukv.reshape(KV_LORA, MLA_HEADS, QK_NOPE + V_DIM)
    wk = jnp.pad(wkv[:, :, :QK_NOPE], ((0, 0), (0, 0), (0, QK_PAD - QK_NOPE)))
    wk = wk.reshape(KV_LORA, MLA_HEADS * QK_PAD).astype(BF16)
    wvt = jnp.pad(wkv[:, :, QK_NOPE:], ((0, 0), (0, 0), (0, VT_ROWS - V_DIM)))
    wvt = wvt.reshape(KV_LORA, MLA_HEADS * VT_ROWS).T.astype(BF16)
    ones_col = ((jnp.arange(MLA_HEADS * VT_ROWS) % VT_ROWS) == V_DIM).astype(F32)[:, None]

    tm = _tile(seq, TOKEN_TILE)
    u2 = _rms_matmul(x2d, row(norm_g), w_rest, _tile(n, PROJ_ROWS), _tile(w_rest.shape[1], PROJ_COLS),
                     BF16)
    qt, k, vt = _mla_prep(x2d, row(norm_g), wa, row(q_g), row(kv_g), wqt, wk, wvt, ones_col,
                          *tables, batch, seq, _tile(seq, PREP_TILE), tm)
    ot = _attention(qt, k, vt, batch, seq, _tile(seq, QUERY_TILE), tm)
    oa = _conformer(u2, conv_w.astype(F32), row(conv_b), row(ln_g), row(ln_b), batch, seq,
                    _tile(seq, CONV_TILE))
    w_out = w_out.astype(BF16)
    return _out_even(x2d, ot, u2, oa, w_out[:MLA_W], w_out[MLA_W:], batch, seq, _tile(seq, OUT_ROWS))


def _odd_layer(x2d, batch, seq, norm_g, w_in, conv_c_w, conv_c_b, dt_bias_f, dt_bias_b, a_log_f, a_log_b,
               d_skip, ssd_g, conv_d_w, w_out, final_g):
    n = batch * seq
    row = lambda v: v.reshape(1, -1).astype(F32)
    o_dt = SSD_W + XBC_W
    w_main = jnp.concatenate([w_in[:, :o_dt], w_in[:, o_dt + 2 * SSD_HEADS:]], axis=1).astype(BF16)
    w_dt = w_in[:, o_dt:o_dt + 2 * SSD_HEADS].reshape(D_MODEL, 2 * SSD_GROUPS, SSD_HPG)
    wdt_t = jnp.pad(w_dt, ((0, 0), (0, 0), (0, 8 - SSD_HPG))).reshape(D_MODEL, 64).T.astype(BF16)

    def per_head(vf, vb):
        v = jnp.concatenate([vf, vb]).reshape(2, SSD_GROUPS, SSD_HPG)
        v = jnp.pad(v, ((0, 0), (0, 0), (0, 8 - SSD_HPG)))
        return jnp.broadcast_to(v[..., None], (2, SSD_GROUPS, 8, SSD_CHUNK)).astype(F32)

    tm = _tile(seq, TOKEN_TILE)
    uo, dt_t = _rms_matmul(x2d, row(norm_g), w_main, _tile(n, PROJ_ROWS), _tile(w_main.shape[1], PROJ_COLS),
                           BF16, wt=wdt_t)
    dt_t = dt_t.reshape(2, SSD_GROUPS, 8, n)
    xbc = _xbc_conv(uo, conv_c_w.astype(F32), row(conv_c_b), batch, seq, _tile(seq, NARROW_CONV_TILE), 512,
                    SSD_W // 512)
    y2 = _ssd(xbc, dt_t, per_head(dt_bias_f, dt_bias_b), per_head(a_log_f, a_log_b), batch, seq)
    dskip = row(jnp.repeat(d_skip, SSD_HEAD_DIM))
    od = _short_conv(uo, conv_d_w.astype(F32), batch, seq, _tile(seq, NARROW_CONV_TILE), (SSD_W + XBC_W) // SC_W)
    w_out = w_out.astype(BF16)
    return _out_odd(x2d, y2, xbc, uo, dskip, row(ssd_g), od, w_out[:SSD_W], w_out[SSD_W:], final_g, tm)


def _rope_tables(seq):
    half = QK_ROPE // 2
    inv = ROPE_THETA ** (-jnp.arange(half, dtype=F32) / half)
    ang = jnp.arange(seq, dtype=F32)[:, None] * inv[None, :]
    cos, sin = jnp.cos(ang), jnp.sin(ang)
    pad = ((0, 0), (0, LANES - QK_ROPE))
    cos2 = jnp.pad(jnp.concatenate([cos, cos], axis=1), pad)
    sin2 = jnp.pad(jnp.concatenate([-sin, sin], axis=1), pad)
    return cos2, sin2, cos.T, sin.T


def kernel(x, norm_e, w_in_e, w_uq, q_norm, w_ukv, kv_norm, conv_a_w, conv_a_b, ln_a_g, ln_a_b, w_out_e, norm_o, w_in_o, conv_c_w, conv_c_b, dt_bias_f, dt_bias_b, a_log_f, a_log_b, d_skip, ssd_norm, conv_d_w, w_out_o, final_norm):
    batch, seq, _ = x.shape
    depth = norm_e.shape[0] + norm_o.shape[0]
    tables = _rope_tables(seq)
    x2d = x.reshape(batch * seq, D_MODEL)
    for i in range(depth):
        j = i // 2
        if i % 2 == 0:
            x2d = _even_layer(x2d, batch, seq, norm_e[j], w_in_e[j], w_uq[j], q_norm[j], w_ukv[j], kv_norm[j],
                              conv_a_w[j], conv_a_b[j], ln_a_g[j], ln_a_b[j], w_out_e[j], tables)
        else:
            last = i == depth - 1
            x2d = _odd_layer(x2d, batch, seq, norm_o[j], w_in_o[j], conv_c_w[j], conv_c_b[j], dt_bias_f[j],
                             dt_bias_b[j], a_log_f[j], a_log_b[j], d_skip[j], ssd_norm[j], conv_d_w[j],
                             w_out_o[j], final_norm if last else None)
    if depth % 2 == 1:
        x2d = _final_norm(x2d, final_norm.reshape(1, -1), _tile(batch * seq, TOKEN_TILE))
    return x2d.reshape(batch, seq, D_MODEL)
```

```python
import functools
import math

import jax
import jax.numpy as jnp
from jax import lax
from jax.experimental import pallas as pl
from jax.experimental.pallas import tpu as pltpu

F32 = jnp.float32
BF16 = jnp.bfloat16
EPS = 1e-6

D_MODEL = 1024
MLA_HEADS = 16
QK_NOPE = 64
QK_ROPE = 32
V_DIM = 64
Q_LORA = 384
KV_LORA = 256
MLA_W = MLA_HEADS * V_DIM
ROPE_THETA = 10000.0
CONV_W = 1024
CONV_K = 31
SSD_HEAD_DIM = 64
SSD_HEADS = 24
SSD_W = SSD_HEADS * SSD_HEAD_DIM
SSD_GROUPS = 4
SSD_HPG = SSD_HEADS // SSD_GROUPS
SSD_STATE = 128
SSD_CONV_K = 5
SSD_CHUNK = 128
XBC_W = SSD_W + 2 * SSD_GROUPS * SSD_STATE
SC_W = 512
SC_K = 3

LANES = 128
HALO = 16
QK_PAD = 128
VT_ROWS = 128
ATTN_SAFE_MAX = 1e30
ATTN_QK_GROUP = 2
VMEM_LIMIT = 56 * 1024 * 1024

TOKEN_TILE = 512
PREP_TILE = 1024
CONV_TILE = 1024
NARROW_CONV_TILE = 2048
QUERY_TILE = 2048
PROJ_ROWS = 1024
PROJ_COLS = 2048
OUT_ROWS = 1024
SSD_CHUNKS_PER_STEP = 8


def _cparams(sem):
    return pltpu.CompilerParams(dimension_semantics=sem, vmem_limit_bytes=VMEM_LIMIT)


def _silu(v):
    return v * (1.0 / (1.0 + jnp.exp(-v)))


def _softplus(v):
    return jnp.maximum(v, 0.0) + jnp.log(1.0 + jnp.exp(-jnp.abs(v)))


def _rms(v, g):
    ms = jnp.mean(v * v, axis=-1, keepdims=True)
    return v * lax.rsqrt(ms + EPS) * g


def _rms_matmul_kernel(x_ref, g_ref, w_ref, *rest):
    o_ref, xn_ref = rest[-3 if len(rest) == 4 else -2], rest[-1]

    @pl.when(pl.program_id(1) == 0)
    def _():
        xn_ref[...] = _rms(x_ref[...], g_ref[...]).astype(xn_ref.dtype)
        if len(rest) == 4:
            rest[2][...] = lax.dot_general(rest[0][...], xn_ref[...], (((1,), (1,)), ((), ())),
                                           preferred_element_type=F32)

    o_ref[...] = jnp.dot(xn_ref[...], w_ref[...], preferred_element_type=F32).astype(o_ref.dtype)


def _rms_matmul(x, g, w, tm, tn, out_dtype, wt=None):
    n, d = x.shape
    e = w.shape[1]
    in_specs = [
        pl.BlockSpec((tm, d), lambda i, j: (i, 0)),
        pl.BlockSpec((1, d), lambda i, j: (0, 0)),
        pl.BlockSpec((d, tn), lambda i, j: (0, j)),
    ]
    out_specs = [pl.BlockSpec((tm, tn), lambda i, j: (i, j))]
    out_shape = [jax.ShapeDtypeStruct((n, e), out_dtype)]
    args = [x, g, w]
    if wt is not None:
        in_specs.append(pl.BlockSpec(wt.shape, lambda i, j: (0, 0)))
        out_specs.append(pl.BlockSpec((wt.shape[0], tm), lambda i, j: (0, i)))
        out_shape.append(jax.ShapeDtypeStruct((wt.shape[0], n), F32))
        args.append(wt)
    outs = pl.pallas_call(
        _rms_matmul_kernel,
        grid=(n // tm, e // tn),
        in_specs=in_specs,
        out_specs=out_specs,
        out_shape=out_shape,
        scratch_shapes=[pltpu.VMEM((tm, d), BF16)],
        compiler_params=_cparams(("parallel", "arbitrary")),
        name="rms_matmul",
    )(*args)
    return outs[0] if wt is None else outs


def _mla_prep_kernel(x_ref, g_ref, wa_ref, qg_ref, kvg_ref, wqt_ref, wk_ref, wvt_ref,
                     ones_ref, cos2_ref, sin2_ref, cost_ref, sint_ref,
                     qt_ref, k_ref, vt_ref, *, qscale):
    xn = _rms(x_ref[...], g_ref[...]).astype(BF16)
    a = jnp.dot(xn, wa_ref[...], preferred_element_type=F32)
    qn = _rms(a[:, :Q_LORA], qg_ref[...]).astype(BF16)
    kvn = _rms(a[:, Q_LORA:Q_LORA + KV_LORA], kvg_ref[...]).astype(BF16)

    kr = a[:, Q_LORA + KV_LORA:]
    lane = lax.broadcasted_iota(jnp.int32, kr.shape, 1)
    half = QK_ROPE // 2
    swapped = jnp.where(lane < half, pltpu.roll(kr, LANES - half, 1), pltpu.roll(kr, half, 1))
    kr = kr * cos2_ref[...] + swapped * sin2_ref[...]

    k = jnp.dot(kvn, wk_ref[...], preferred_element_type=F32)
    k = k + jnp.tile(pltpu.roll(kr, QK_NOPE, 1), (1, MLA_HEADS))
    k_ref[...] = k.astype(k_ref.dtype)

    nt = (((1,), (1,)), ((), ()))
    vt = lax.dot_general(wvt_ref[...], kvn, nt, preferred_element_type=F32) + ones_ref[...]
    tk = vt_ref.shape[-1]
    for c in range(vt_ref.shape[0]):
        vt_ref[c] = vt[:, c * tk:(c + 1) * tk].astype(vt_ref.dtype)

    qt = lax.dot_general(wqt_ref[...], qn, nt, preferred_element_type=F32) * qscale
    qt_ref[...] = qt.astype(qt_ref.dtype)
    cos_t = cost_ref[...]
    sin_t = sint_ref[...]
    for h in range(MLA_HEADS):
        r0 = h * QK_PAD + QK_NOPE
        x1 = qt[r0:r0 + half]
        x2 = qt[r0 + half:r0 + 2 * half]
        qt_ref[r0:r0 + half, :] = (x1 * cos_t - x2 * sin_t).astype(qt_ref.dtype)
        qt_ref[r0 + half:r0 + 2 * half, :] = (x2 * cos_t + x1 * sin_t).astype(qt_ref.dtype)


def _mla_prep(x2d, g, wa, qg, kvg, wqt, wk, wvt, ones_col, cos2, sin2, cos_t, sin_t,
              batch, seq, tm, tk):
    nblk = seq // tm
    qscale = (QK_NOPE + QK_ROPE) ** -0.5 * math.log2(math.e)
    full = lambda arr: pl.BlockSpec(arr.shape, lambda b, i: (0,) * arr.ndim)
    hq = MLA_HEADS * QK_PAD
    hv = MLA_HEADS * VT_ROWS
    return pl.pallas_call(
        functools.partial(_mla_prep_kernel, qscale=qscale),
        grid=(batch, nblk),
        in_specs=[
            pl.BlockSpec((tm, D_MODEL), lambda b, i: (b * nblk + i, 0)),
            full(g), full(wa), full(qg), full(kvg), full(wqt), full(wk), full(wvt),
            full(ones_col),
            pl.BlockSpec((tm, LANES), lambda b, i: (i, 0)),
            pl.BlockSpec((tm, LANES), lambda b, i: (i, 0)),
            pl.BlockSpec((QK_ROPE // 2, tm), lambda b, i: (0, i)),
            pl.BlockSpec((QK_ROPE // 2, tm), lambda b, i: (0, i)),
        ],
        out_specs=[
            pl.BlockSpec((None, hq, tm), lambda b, i: (b, 0, i)),
            pl.BlockSpec((None, tm, hq), lambda b, i: (b, i, 0)),
            pl.BlockSpec((None, tm // tk, hv, tk), lambda b, i: (b, i, 0, 0)),
        ],
        out_shape=[
            jax.ShapeDtypeStruct((batch, hq, seq), BF16),
            jax.ShapeDtypeStruct((batch, seq, hq), BF16),
            jax.ShapeDtypeStruct((batch, seq // tk, hv, tk), BF16),
        ],
        compiler_params=_cparams(("parallel", "parallel")),
        name="mla_prep",
    )(x2d, g, wa, qg, kvg, wqt, wk, wvt, ones_col, cos2, sin2, cos_t, sin_t)


def _attn_kernel(qt_ref, qn_ref, k_ref, vt_ref, o_ref, m_ref, acc_ref, s0_ref, c0_ref, *, tk, nk):
    def first_scores(q_ref):
        st = jnp.dot(k_ref[0:tk, :], q_ref[...], preferred_element_type=F32)
        s0_ref[...] = st
        c0_ref[...] = jnp.max(st, axis=0, keepdims=True)

    @pl.when(pl.program_id(2) == 0)
    def _():
        first_scores(qt_ref)

    qt = qt_ref[...]
    m0 = c0_ref[...]
    acc = jnp.dot(vt_ref[0], jnp.exp2(s0_ref[...] - m0).astype(BF16), preferred_element_type=F32)
    c = 1
    while c < nk:
        c_end = min(nk, (c // ATTN_QK_GROUP + 1) * ATTN_QK_GROUP)
        st = jnp.dot(k_ref[c * tk:c_end * tk, :], qt, preferred_element_type=F32)
        for u in range(c_end - c):
            p = jnp.exp2(st[u * tk:(u + 1) * tk] - m0).astype(BF16)
            acc = acc + jnp.dot(vt_ref[c + u], p, preferred_element_type=F32)
        c = c_end
    acc_ref[...] = acc
    first_scores(qn_ref)

    used = acc[:V_DIM + 8]
    unsafe = jnp.max(jnp.where(jnp.abs(used) < ATTN_SAFE_MAX, 0.0, 1.0)) > 0.0

    @pl.when(unsafe)
    def _():
        m_ref[...] = jnp.full(m_ref.shape, -jnp.inf, F32)
        acc_ref[...] = jnp.zeros(acc_ref.shape, F32)

        def body(c, carry):
            k0 = pl.multiple_of(c * tk, tk)
            st = jnp.dot(k_ref[pl.ds(k0, tk), :], qt, preferred_element_type=F32)
            m_old = m_ref[...]
            m_new = jnp.maximum(m_old, jnp.max(st, axis=0, keepdims=True))
            p = jnp.exp2(st - m_new).astype(BF16)
            pv = jnp.dot(vt_ref[c], p, preferred_element_type=F32)
            acc_ref[...] = acc_ref[...] * jnp.exp2(m_old - m_new) + pv
            m_ref[...] = m_new
            return carry

        lax.fori_loop(0, nk, body, 0)

    acc = acc_ref[...]
    o_ref[...] = (acc[:V_DIM] * (1.0 / acc[V_DIM:V_DIM + 1])).astype(o_ref.dtype)


def _attention(qt, k, vt, batch, seq, tq, tk):
    nk = seq // tk
    nq = seq // tq
    return pl.pallas_call(
        functools.partial(_attn_kernel, tk=tk, nk=nk),
        grid=(batch, MLA_HEADS, nq),
        in_specs=[
            pl.BlockSpec((None, QK_PAD, tq), lambda b, h, i: (b, h, i)),
            pl.BlockSpec((None, QK_PAD, tq), lambda b, h, i: (b, h, jnp.minimum(i + 1, nq - 1))),
            pl.BlockSpec((None, seq, QK_PAD), lambda b, h, i: (b, 0, h)),
            pl.BlockSpec((None, nk, VT_ROWS, tk), lambda b, h, i: (b, 0, h, 0)),
        ],
        out_specs=pl.BlockSpec((None, V_DIM, tq), lambda b, h, i: (b, h, i)),
        out_shape=jax.ShapeDtypeStruct((batch, MLA_W, seq), BF16),
        scratch_shapes=[pltpu.VMEM((1, tq), F32), pltpu.VMEM((VT_ROWS, tq), F32),
                        pltpu.VMEM((tk, tq), F32), pltpu.VMEM((1, tq), F32)],
        compiler_params=_cparams(("parallel", "parallel", "arbitrary")),
        name="mla_attention",
    )(qt, qt, k, vt)


def _fill_ext(ext_ref, prev, cur, nxt, i, n_i, ts):
    ext_ref[0:HALO, :] = jnp.where(i > 0, prev, 0.0)
    ext_ref[HALO:HALO + ts, :] = cur
    ext_ref[HALO + ts:2 * HALO + ts, :] = jnp.where(i < n_i - 1, nxt, 0.0)


def _dwconv(ext_ref, w_ref, k_taps, ts, width, emit, rows=128):
    def body(rc, carry):
        r0 = pl.multiple_of(rc * rows, rows)
        for cb in range(width // LANES):
            sl = slice(cb * LANES, (cb + 1) * LANES)
            emit(r0, sl, _dwconv_block(ext_ref, w_ref, k_taps, r0, rows, sl))
        return carry

    lax.fori_loop(0, ts // rows, body, 0)


def _dwconv_block(ext_ref, w_ref, k_taps, r0, rows, sl):
    sub = 8
    off = HALO - k_taps // 2
    lo = (off // sub) * sub
    win_rows = ((off - lo + k_taps - 1) // sub + 1) * sub + rows
    win = ext_ref[pl.ds(r0 + lo, win_rows), sl]
    acc = jnp.zeros((rows, LANES), F32)
    for phase in range(sub):
        taps = [k for k in range(k_taps) if (off - lo + k) % sub == phase]
        if not taps:
            continue
        shifted = win if phase == 0 else pltpu.roll(win, win_rows - phase, 0)
        for k in taps:
            q = (off - lo + k) // sub * sub
            acc = acc + w_ref[k:k + 1, sl] * shifted[q:q + rows]
    return acc


def _halo_specs(ts, width, col, nblk, batch):
    per = ts // HALO
    last = batch * nblk * per - 1

    def prev_map(b, i, *_):
        return (jnp.maximum((b * nblk + i) * per - 1, 0), col(*_))

    def cur_map(b, i, *_):
        return (b * nblk + i, col(*_))

    def next_map(b, i, *_):
        return (jnp.minimum((b * nblk + i + 1) * per, last), col(*_))

    return [pl.BlockSpec((HALO, width), prev_map), pl.BlockSpec((ts, width), cur_map),
            pl.BlockSpec((HALO, width), next_map)]


def _conformer_kernel(ap_ref, a_ref, an_ref, gp_ref, g_ref, gn_ref, z_ref, w_ref, b_ref, lng_ref,
                      lnb_ref, o_ref, ext_ref, conv_ref, *, ts):
    i = pl.program_id(1)
    n_i = pl.num_programs(1)

    def glu(a, g):
        return a.astype(F32) * (1.0 / (1.0 + jnp.exp(-g.astype(F32))))

    _fill_ext(ext_ref, glu(ap_ref[...], gp_ref[...]), glu(a_ref[...], g_ref[...]),
              glu(an_ref[...], gn_ref[...]), i, n_i, ts)

    def emit(r0, sl, acc):
        conv_ref[pl.ds(r0, acc.shape[0]), sl] = acc + b_ref[:, sl]

    _dwconv(ext_ref, w_ref, CONV_K, ts, CONV_W, emit)

    rows = 128

    def ln_body(rc, carry):
        r0 = pl.multiple_of(rc * rows, rows)
        v = conv_ref[pl.ds(r0, rows), :]
        vc = v - jnp.mean(v, axis=-1, keepdims=True)
        var = jnp.mean(vc * vc, axis=-1, keepdims=True)
        y = vc * lax.rsqrt(var + EPS) * lng_ref[...] + lnb_ref[...]
        y = _silu(y) * _silu(z_ref[pl.ds(r0, rows), :].astype(F32))
        o_ref[pl.ds(r0, rows), :] = y.astype(o_ref.dtype)
        return carry

    lax.fori_loop(0, ts // rows, ln_body, 0)


def _conformer(u2, w, b, lng, lnb, batch, seq, ts):
    n = batch * seq
    nblk = seq // ts
    a_specs = _halo_specs(ts, CONV_W, lambda: 1, nblk, batch)
    g_specs = _halo_specs(ts, CONV_W, lambda: 2, nblk, batch)
    small = lambda arr: pl.BlockSpec(arr.shape, lambda b_, i: (0, 0))
    return pl.pallas_call(
        functools.partial(_conformer_kernel, ts=ts),
        grid=(batch, nblk),
        in_specs=a_specs + g_specs + [
            pl.BlockSpec((ts, CONV_W), lambda b_, i: (b_ * nblk + i, 3)),
            small(w), small(b), small(lng), small(lnb)],
        out_specs=pl.BlockSpec((ts, CONV_W), lambda b_, i: (b_ * nblk + i, 0)),
        out_shape=jax.ShapeDtypeStruct((n, CONV_W), BF16),
        scratch_shapes=[pltpu.VMEM((ts + 2 * HALO, CONV_W), F32), pltpu.VMEM((ts, CONV_W), F32)],
        compiler_params=_cparams(("parallel", "parallel")),
        name="conformer_conv",
    )(u2, u2, u2, u2, u2, u2, u2, w, b, lng, lnb)


def _out_even_kernel(x_ref, ot_ref, z_ref, oa_ref, w1_ref, w2_ref, o_ref):
    ob = ot_ref[...].astype(F32).T * _silu(z_ref[...].astype(F32))
    acc = jnp.dot(ob.astype(BF16), w1_ref[...], preferred_element_type=F32)
    acc = acc + jnp.dot(oa_ref[...], w2_ref[...], preferred_element_type=F32)
    o_ref[...] = x_ref[...] + acc


def _out_even(x2d, ot, u2, oa, w1, w2, batch, seq, tm):
    n = batch * seq
    nblk = seq // tm
    return pl.pallas_call(
        _out_even_kernel,
        grid=(batch, nblk),
        in_specs=[
            pl.BlockSpec((tm, D_MODEL), lambda b, i: (b * nblk + i, 0)),
            pl.BlockSpec((None, MLA_W, tm), lambda b, i: (b, 0, i)),
            pl.BlockSpec((tm, MLA_W), lambda b, i: (b * nblk + i, 0)),
            pl.BlockSpec((tm, CONV_W), lambda b, i: (b * nblk + i, 0)),
            pl.BlockSpec(w1.shape, lambda b, i: (0, 0)),
            pl.BlockSpec(w2.shape, lambda b, i: (0, 0)),
        ],
        out_specs=pl.BlockSpec((tm, D_MODEL), lambda b, i: (b * nblk + i, 0)),
        out_shape=jax.ShapeDtypeStruct((n, D_MODEL), F32),
        compiler_params=_cparams(("parallel", "parallel")),
        name="out_proj_even",
    )(x2d, ot, u2, oa, w1, w2)


def _xbc_conv_kernel(p_ref, c_ref, n_ref, w_ref, b_ref, o_ref, ext_ref, *, ts, width):
    i = pl.program_id(1)
    _fill_ext(ext_ref, p_ref[...].astype(F32), c_ref[...].astype(F32), n_ref[...].astype(F32),
              i, pl.num_programs(1), ts)

    def emit(r0, sl, acc):
        o_ref[pl.ds(r0, acc.shape[0]), sl] = _silu(acc + b_ref[:, sl]).astype(o_ref.dtype)

    _dwconv(ext_ref, w_ref, SSD_CONV_K, ts, width, emit)


def _xbc_conv(uo, w, b, batch, seq, ts, width, col0):
    n = batch * seq
    nblk = seq // ts
    ncol = XBC_W // width
    specs = _halo_specs(ts, width, lambda j: col0 + j, nblk, batch)
    return pl.pallas_call(
        functools.partial(_xbc_conv_kernel, ts=ts, width=width),
        grid=(batch, nblk, ncol),
        in_specs=specs + [pl.BlockSpec((SSD_CONV_K, width), lambda b_, i, j: (0, j)),
                          pl.BlockSpec((1, width), lambda b_, i, j: (0, j))],
        out_specs=pl.BlockSpec((ts, width), lambda b_, i, j: (b_ * nblk + i, j)),
        out_shape=jax.ShapeDtypeStruct((n, XBC_W), BF16),
        scratch_shapes=[pltpu.VMEM((ts + 2 * HALO, width), F32)],
        compiler_params=_cparams(("parallel", "parallel", "parallel")),
        name="xbc_conv",
    )(uo, uo, uo, w, b)


def _ssd_kernel(xbc_ref, dt_ref, bias_ref, alog_ref, y_ref, state_ref, lhs_ref, btw_ref, *, cps):
    d = pl.program_id(1)
    c = pl.program_id(2)
    L = SSD_CHUNK
    G = SSD_GROUPS
    pairs = SSD_HPG // 2
    gw = SSD_HPG * SSD_HEAD_DIM

    @pl.when(c == 0)
    def _():
        state_ref[...] = jnp.zeros(state_ref.shape, F32)

    row = lax.broadcasted_iota(jnp.int32, (L, L), 0)
    col = lax.broadcasted_iota(jnp.int32, (L, L), 1)
    lane_lo = col < SSD_HEAD_DIM
    ahead = (row - col) * (1 - 2 * d)
    mask = ahead >= 0
    tri = jnp.where(ahead <= 0, 1.0, 0.0)
    stack_rows = lambda ref: jnp.concatenate([ref[g] for g in range(G)], axis=0)
    bias = stack_rows(bias_ref)
    neg_a = -math.log2(math.e) * jnp.exp(stack_rows(alog_ref))
    dt_all = stack_rows(dt_ref)

    def operands(k, pos):
        rows = pl.ds(pl.multiple_of(pos * L, L), L)
        dt_raw = dt_all[:, 0:L]
        for q in range(1, cps):
            dt_raw = jnp.where(pos == q, dt_all[:, q * L:(q + 1) * L], dt_raw)
        dtv = _softplus(dt_raw + bias)
        da = dtv * neg_a
        cs_t = jnp.dot(da, tri, preferred_element_type=F32, precision=lax.Precision.HIGHEST)
        tot = jnp.sum(da, axis=1, keepdims=True)
        etot = jnp.exp2(jnp.broadcast_to(tot, cs_t.shape))
        w_t = jnp.exp2(tot - cs_t) * dtv
        srow_t = cs_t - jnp.log2(dtv)
        cs_cols = jnp.concatenate([cs_t, jnp.zeros((L - 8 * G, L), F32)], axis=0).T
        for g in range(G):
            bm = xbc_ref[rows, SSD_W + g * SSD_STATE:SSD_W + (g + 1) * SSD_STATE]
            cm = xbc_ref[rows, SSD_W + (G + g) * SSD_STATE:SSD_W + (G + g + 1) * SSD_STATE]
            cb = lax.dot_general(cm, bm, (((1,), (1,)), ((), ())), preferred_element_type=F32)
            bt = bm.astype(F32).T
            cmf = cm.astype(F32)
            for rr in range(SSD_HPG):
                r = 8 * g + rr
                i = g * SSD_HPG + rr
                cs_col = jnp.broadcast_to(cs_cols[:, r:r + 1], (L, L))
                m = jnp.where(mask, jnp.exp2(cs_col - srow_t[r:r + 1, :]), 0.0) * cb
                lhs_ref[k, i, :, 0:L] = m.astype(BF16)
                lhs_ref[k, i, :, L:2 * L] = (cmf * jnp.exp2(cs_col)).astype(BF16)
                btw_ref[k, i] = (bt * w_t[r:r + 1, :]).astype(BF16)
        return etot

    def outputs(k, pos, etot):
        rows = pl.ds(pl.multiple_of(pos * L, L), L)
        for g in range(G):
            for j in range(pairs):
                lanes = slice(g * gw + j * LANES, g * gw + (j + 1) * LANES)
                xs_pair = xbc_ref[rows, lanes]
                st_pair = state_ref[g * pairs + j]
                rhs = jnp.concatenate([xs_pair, st_pair.astype(BF16)], axis=0)
                i0 = g * SSD_HPG + 2 * j
                ys = [jnp.dot(lhs_ref[k, i0 + u], rhs, preferred_element_type=F32) for u in range(2)]
                news = [jnp.dot(btw_ref[k, i0 + u], xs_pair, preferred_element_type=F32) for u in range(2)]
                y_ref[rows, lanes] = jnp.where(lane_lo, ys[0], ys[1]).astype(y_ref.dtype)
                r0 = 8 * g + 2 * j
                decay = jnp.where(lane_lo[:1], etot[r0:r0 + 1, :], etot[r0 + 1:r0 + 2, :])
                state_ref[g * pairs + j] = st_pair * decay + jnp.where(lane_lo, news[0], news[1])

    positions = [jnp.where(d == 0, k, cps - 1 - k) for k in range(cps)]
    decays = [operands(k, positions[k]) for k in range(cps)]
    for k in range(cps):
        outputs(k, positions[k], decays[k])


def _ssd(xbc, dt_t, bias, alog, batch, seq):
    n = batch * seq
    nc = seq // SSD_CHUNK
    cps = SSD_CHUNKS_PER_STEP if nc % SSD_CHUNKS_PER_STEP == 0 else 1
    nsteps = nc // cps
    rows = cps * SSD_CHUNK

    def tpos(b, d, c):
        return b * nsteps + jnp.where(d == 0, c, nsteps - 1 - c)

    head_rows = (None, SSD_GROUPS, 8, SSD_CHUNK)
    return pl.pallas_call(
        functools.partial(_ssd_kernel, cps=cps),
        grid=(batch, 2, nsteps),
        in_specs=[
            pl.BlockSpec((rows, XBC_W), lambda b, d, c: (tpos(b, d, c), 0)),
            pl.BlockSpec((None, SSD_GROUPS, 8, rows), lambda b, d, c: (d, 0, 0, tpos(b, d, c))),
            pl.BlockSpec(head_rows, lambda b, d, c: (d, 0, 0, 0)),
            pl.BlockSpec(head_rows, lambda b, d, c: (d, 0, 0, 0)),
        ],
        out_specs=pl.BlockSpec((None, rows, SSD_W), lambda b, d, c: (d, tpos(b, d, c), 0)),
        out_shape=jax.ShapeDtypeStruct((2, n, SSD_W), BF16),
        scratch_shapes=[pltpu.VMEM((SSD_GROUPS * SSD_HPG // 2, SSD_STATE, LANES), F32),
                        pltpu.VMEM((cps, SSD_HEADS, SSD_CHUNK, 2 * SSD_CHUNK), BF16),
                        pltpu.VMEM((cps, SSD_HEADS, SSD_STATE, SSD_CHUNK), BF16)],
        compiler_params=_cparams(("parallel", "arbitrary", "arbitrary")),
        name="ssd_scan",
    )(xbc, dt_t, bias, alog)


def _short_conv_kernel(gcp_ref, gc_ref, gcn_ref, hp_ref, h_ref, hn_ref, gb_ref, z_ref, w_ref,
                       o_ref, ext_ref, *, ts):
    i = pl.program_id(1)
    mul = lambda a, b: a.astype(F32) * b.astype(F32)
    _fill_ext(ext_ref, mul(gcp_ref[...], hp_ref[...]), mul(gc_ref[...], h_ref[...]),
              mul(gcn_ref[...], hn_ref[...]), i, pl.num_programs(1), ts)

    def emit(r0, sl, acc):
        rs = pl.ds(r0, acc.shape[0])
        y = gb_ref[rs, sl].astype(F32) * acc * _silu(z_ref[rs, sl].astype(F32))
        o_ref[rs, sl] = y.astype(o_ref.dtype)

    _dwconv(ext_ref, w_ref, SC_K, ts, SC_W, emit)


def _short_conv(uo, w, batch, seq, ts, col0):
    n = batch * seq
    nblk = seq // ts
    gc_specs = _halo_specs(ts, SC_W, lambda: col0 + 1, nblk, batch)
    h_specs = _halo_specs(ts, SC_W, lambda: col0 + 2, nblk, batch)
    return pl.pallas_call(
        functools.partial(_short_conv_kernel, ts=ts),
        grid=(batch, nblk),
        in_specs=gc_specs + h_specs + [
            pl.BlockSpec((ts, SC_W), lambda b, i: (b * nblk + i, col0)),
            pl.BlockSpec((ts, SC_W), lambda b, i: (b * nblk + i, col0 + 3)),
            pl.BlockSpec(w.shape, lambda b, i: (0, 0))],
        out_specs=pl.BlockSpec((ts, SC_W), lambda b, i: (b * nblk + i, 0)),
        out_shape=jax.ShapeDtypeStruct((n, SC_W), BF16),
        scratch_shapes=[pltpu.VMEM((ts + 2 * HALO, SC_W), F32)],
        compiler_params=_cparams(("parallel", "parallel")),
        name="short_conv",
    )(uo, uo, uo, uo, uo, uo, uo, uo, w)


def _out_odd_kernel(x_ref, y_ref, xs_ref, z_ref, dskip_ref, g_ref, od_ref, w1_ref, w2_ref, *rest):
    o_ref = rest[-1]
    gw = SSD_HPG * SSD_HEAD_DIM
    y = y_ref[0].astype(F32) + y_ref[1].astype(F32) + dskip_ref[...] * xs_ref[...].astype(F32)
    y = y * _silu(z_ref[...].astype(F32))
    oc = jnp.concatenate([_rms(y[:, g * gw:(g + 1) * gw], g_ref[:, g * gw:(g + 1) * gw]).astype(BF16)
                          for g in range(SSD_GROUPS)], axis=1)
    acc = jnp.dot(oc, w1_ref[...], preferred_element_type=F32)
    acc = acc + jnp.dot(od_ref[...], w2_ref[...], preferred_element_type=F32)
    out = x_ref[...] + acc
    if len(rest) == 2:
        out = _rms(out, rest[0][...])
    o_ref[...] = out


def _out_odd(x2d, y2, xbc, uo, dskip, ssd_g, od, w1, w2, final_g, tm):
    n = x2d.shape[0]
    extra = [] if final_g is None else [final_g.reshape(1, -1).astype(F32)]
    return pl.pallas_call(
        _out_odd_kernel,
        grid=(n // tm,),
        in_specs=[
            pl.BlockSpec((tm, D_MODEL), lambda i: (i, 0)),
            pl.BlockSpec((2, tm, SSD_W), lambda i: (0, i, 0)),
            pl.BlockSpec((tm, SSD_W), lambda i: (i, 0)),
            pl.BlockSpec((tm, SSD_W), lambda i: (i, 0)),
            pl.BlockSpec((1, SSD_W), lambda i: (0, 0)),
            pl.BlockSpec((1, SSD_W), lambda i: (0, 0)),
            pl.BlockSpec((tm, SC_W), lambda i: (i, 0)),
            pl.BlockSpec(w1.shape, lambda i: (0, 0)),
            pl.BlockSpec(w2.shape, lambda i: (0, 0)),
        ] + [pl.BlockSpec((1, D_MODEL), lambda i: (0, 0)) for _ in extra],
        out_specs=pl.BlockSpec((tm, D_MODEL), lambda i: (i, 0)),
        out_shape=jax.ShapeDtypeStruct((n, D_MODEL), F32),
        compiler_params=_cparams(("parallel",)),
        name="out_proj_odd",
    )(x2d, y2, xbc, uo, dskip, ssd_g, od, w1, w2, *extra)


def _final_norm_kernel(x_ref, g_ref, o_ref):
    o_ref[...] = _rms(x_ref[...], g_ref[...])


def _final_norm(x2d, g, tm):
    n = x2d.shape[0]
    return pl.pallas_call(
        _final_norm_kernel,
        grid=(n // tm,),
        in_specs=[pl.BlockSpec((tm, D_MODEL), lambda i: (i, 0)), pl.BlockSpec((1, D_MODEL), lambda i: (0, 0))],
        out_specs=pl.BlockSpec((tm, D_MODEL), lambda i: (i, 0)),
        out_shape=jax.ShapeDtypeStruct((n, D_MODEL), F32),
        compiler_params=_cparams(("parallel",)),
        name="final_norm",
    )(x2d, g)


def _tile(total, want):
    return want if total % want == 0 else total


def _even_layer(x2d, batch, seq, norm_g, w_in, w_uq, q_g, w_ukv, kv_g, conv_w, conv_b, ln_g, ln_b, w_out,
                tables):
    n = batch * seq
    low = Q_LORA + KV_LORA + QK_ROPE
    row = lambda v: v.reshape(1, -1).astype(F32)

    wa = jnp.pad(w_in[:, :low], ((0, 0), (0, 6 * LANES - low))).astype(BF16)
    w_rest = w_in[:, low:].astype(BF16)
    wq = w_uq.reshape(Q_LORA, MLA_HEADS, QK_NOPE + QK_ROPE)
    wqt = jnp.pad(wq, ((0, 0), (0, 0), (0, QK_PAD - QK_NOPE - QK_ROPE)))
    wqt = wqt.reshape(Q_LORA, MLA_HEADS * QK_PAD).T.astype(BF16)
    wkv = w_ukv.reshape(KV_LORA, MLA_HEADS, QK_NOPE + V_DIM)
    wk = jnp.pad(wkv[:, :, :QK_NOPE], ((0, 0), (0, 0), (0, QK_PAD - QK_NOPE)))
    wk = wk.reshape(KV_LORA, MLA_HEADS * QK_PAD).astype(BF16)
    wvt = jnp.pad(wkv[:, :, QK_NOPE:], ((0, 0), (0, 0), (0, VT_ROWS - V_DIM)))
    wvt = wvt.reshape(KV_LORA, MLA_HEADS * VT_ROWS).T.astype(BF16)
    ones_col = ((jnp.arange(MLA_HEADS * VT_ROWS) % VT_ROWS) == V_DIM).astype(F32)[:, None]

    tm = _tile(seq, TOKEN_TILE)
    u2 = _rms_matmul(x2d, row(norm_g), w_rest, _tile(n, PROJ_ROWS), _tile(w_rest.shape[1], PROJ_COLS),
                     BF16)
    qt, k, vt = _mla_prep(x2d, row(norm_g), wa, row(q_g), row(kv_g), wqt, wk, wvt, ones_col,
                          *tables, batch, seq, _tile(seq, PREP_TILE), tm)
    ot = _attention(qt, k, vt, batch, seq, _tile(seq, QUERY_TILE), tm)
    oa = _conformer(u2, conv_w.astype(F32), row(conv_b), row(ln_g), row(ln_b), batch, seq,
                    _tile(seq, CONV_TILE))
    w_out = w_out.astype(BF16)
    return _out_even(x2d, ot, u2, oa, w_out[:MLA_W], w_out[MLA_W:], batch, seq, _tile(seq, OUT_ROWS))


def _odd_layer(x2d, batch, seq, norm_g, w_in, conv_c_w, conv_c_b, dt_bias_f, dt_bias_b, a_log_f, a_log_b,
               d_skip, ssd_g, conv_d_w, w_out, final_g):
    n = batch * seq
    row = lambda v: v.reshape(1, -1).astype(F32)
    o_dt = SSD_W + XBC_W
    w_main = jnp.concatenate([w_in[:, :o_dt], w_in[:, o_dt + 2 * SSD_HEADS:]], axis=1).astype(BF16)
    w_dt = w_in[:, o_dt:o_dt + 2 * SSD_HEADS].reshape(D_MODEL, 2 * SSD_GROUPS, SSD_HPG)
    wdt_t = jnp.pad(w_dt, ((0, 0), (0, 0), (0, 8 - SSD_HPG))).reshape(D_MODEL, 64).T.astype(BF16)

    def per_head(vf, vb):
        v = jnp.concatenate([vf, vb]).reshape(2, SSD_GROUPS, SSD_HPG)
        v = jnp.pad(v, ((0, 0), (0, 0), (0, 8 - SSD_HPG)))
        return jnp.broadcast_to(v[..., None], (2, SSD_GROUPS, 8, SSD_CHUNK)).astype(F32)

    tm = _tile(seq, TOKEN_TILE)
    uo, dt_t = _rms_matmul(x2d, row(norm_g), w_main, _tile(n, PROJ_ROWS), _tile(w_main.shape[1], PROJ_COLS),
                           BF16, wt=wdt_t)
    dt_t = dt_t.reshape(2, SSD_GROUPS, 8, n)
    xbc = _xbc_conv(uo, conv_c_w.astype(F32), row(conv_c_b), batch, seq, _tile(seq, NARROW_CONV_TILE), 512,
                    SSD_W // 512)
    y2 = _ssd(xbc, dt_t, per_head(dt_bias_f, dt_bias_b), per_head(a_log_f, a_log_b), batch, seq)
    dskip = row(jnp.repeat(d_skip, SSD_HEAD_DIM))
    od = _short_conv(uo, conv_d_w.astype(F32), batch, seq, _tile(seq, NARROW_CONV_TILE), (SSD_W + XBC_W) // SC_W)
    w_out = w_out.astype(BF16)
    return _out_odd(x2d, y2, xbc, uo, dskip, row(ssd_g), od, w_out[:SSD_W], w_out[SSD_W:], final_g, tm)


def _rope_tables(seq):
    half = QK_ROPE // 2
    inv = ROPE_THETA ** (-jnp.arange(half, dtype=F32) / half)
    ang = jnp.arange(seq, dtype=F32)[:, None] * inv[None, :]
    cos, sin = jnp.cos(ang), jnp.sin(ang)
    pad = ((0, 0), (0, LANES - QK_ROPE))
    cos2 = jnp.pad(jnp.concatenate([cos, cos], axis=1), pad)
    sin2 = jnp.pad(jnp.concatenate([-sin, sin], axis=1), pad)
    return cos2, sin2, cos.T, sin.T


def kernel(x, norm_e, w_in_e, w_uq, q_norm, w_ukv, kv_norm, conv_a_w, conv_a_b, ln_a_g, ln_a_b, w_out_e, norm_o, w_in_o, conv_c_w, conv_c_b, dt_bias_f, dt_bias_b, a_log_f, a_log_b, d_skip, ssd_norm, conv_d_w, w_out_o, final_norm):
    batch, seq, _ = x.shape
    depth = norm_e.shape[0] + norm_o.shape[0]
    tables = _rope_tables(seq)
    x2d = x.reshape(batch * seq, D_MODEL)
    for i in range(depth):
        j = i // 2
        if i % 2 == 0:
            x2d = _even_layer(x2d, batch, seq, norm_e[j], w_in_e[j], w_uq[j], q_norm[j], w_ukv[j], kv_norm[j],
                              conv_a_w[j], conv_a_b[j], ln_a_g[j], ln_a_b[j], w_out_e[j], tables)
        else:
            last = i == depth - 1
            x2d = _odd_layer(x2d, batch, seq, norm_o[j], w_in_o[j], conv_c_w[j], conv_c_b[j], dt_bias_f[j],
                             dt_bias_b[j], a_log_f[j], a_log_b[j], d_skip[j], ssd_norm[j], conv_d_w[j],
                             w_out_o[j], final_norm if last else None)
    if depth % 2 == 1:
        x2d = _final_norm(x2d, final_norm.reshape(1, -1), _tile(batch * seq, TOKEN_TILE))
    return x2d.reshape(batch, seq, D_MODEL)
```
